```python
import math
import jax
import jax.numpy as jnp
from jax import lax

D_MODEL = 1024
BATCH = 16
SEQ = 2048
DEPTH = 4
DEC_BATCH = 128
DEC_SEQ = 8
PAST_LEN = 8192
PAGE_SIZE = 128

N_EVEN = (DEPTH + 1) // 2
N_ODD = DEPTH // 2
BLOCK_Q = 128
NORM_EPS = 1e-6

RW_HEADS = 8
RW_HEAD_DIM = 64
RW_WIDTH = RW_HEADS * RW_HEAD_DIM
RW_DECAY_LORA = 64
RW_AAA_LORA = 64
RW_GATE_LORA = 128
RW_PROJ = 3 * RW_WIDTH + RW_DECAY_LORA + RW_AAA_LORA + RW_GATE_LORA
RW_GN_EPS = 64e-5

SB_HEADS = 8
SB_HEAD_DIM = 64
SB_WIDTH = SB_HEADS * SB_HEAD_DIM
SB_SCALE = SB_HEAD_DIM ** -0.5
EVEN_IN = RW_PROJ + 3 * SB_WIDTH

S5_GROUPS = 32
S5_GROUP_CH = 16
S5_WIDTH = S5_GROUPS * S5_GROUP_CH
S5_STATE = 64

MLA_HEADS = 8
MLA_Q_RANK = 384
MLA_KV_RANK = 256
MLA_NOPE = 64
MLA_ROPE = 32
MLA_V = 64
MLA_WIDTH = MLA_HEADS * MLA_V
MLA_SCALE = (MLA_NOPE + MLA_ROPE) ** -0.5
ROPE_THETA = 10000.0
ODD_IN = S5_WIDTH + MLA_Q_RANK + MLA_KV_RANK + MLA_ROPE

N_EXPERTS = 32
TOP_K = 4
D_FF_EXPERT = 512
SWIGLU_LIMIT = 7.0
SWIGLU_ALPHA = 1.702

kernel_name = 'hybrid_rwkv7_stickbreak_s5_mla_moe_step'

F32 = jnp.float32


def _rmsnorm(x, g):
    xf = x.astype(F32)
    y = xf * lax.rsqrt(jnp.mean(xf * xf, axis=-1, keepdims=True) + NORM_EPS)
    return (y * g.astype(F32)).astype(x.dtype)


def _split_cols(x, sizes):
    out, off = [], 0
    for s in sizes:
        out.append(x[..., off:off + s])
        off += s
    return out


def _rope(x, pos):
    half = x.shape[-1] // 2
    inv = jnp.power(ROPE_THETA, -jnp.arange(half, dtype=F32) / half)
    ang = pos.astype(F32)[:, None] * inv[None, :]
    shape = (pos.shape[0],) + (1,) * (x.ndim - 3) + (half,)
    cos, sin = jnp.cos(ang).reshape(shape), jnp.sin(ang).reshape(shape)
    xf = x.astype(F32)
    x1, x2 = xf[..., :half], xf[..., half:]
    return jnp.concatenate([x1 * cos - x2 * sin, x2 * cos + x1 * sin], axis=-1).astype(x.dtype)


def _gather_pages(cache, page_table, j):
    pages = cache[page_table, j]
    return pages.reshape((page_table.shape[0], -1) + pages.shape[3:])


def _over_query_blocks(attend, qs, q_pos):
    t = q_pos.shape[0]
    if t <= BLOCK_Q:
        return attend(*qs, q_pos)
    nb = t // BLOCK_Q

    def to_blocks(a):
        return jnp.moveaxis(a.reshape((a.shape[0], nb, BLOCK_Q) + a.shape[2:]), 1, 0)

    out = lax.map(lambda blk: attend(*blk), tuple(to_blocks(a) for a in qs) + (q_pos.reshape(nb, BLOCK_Q),))
    out = jnp.moveaxis(out, 0, 1)
    return out.reshape((out.shape[0], t) + out.shape[3:])


def _stick_breaking(q, k, v, q_pos, k_pos):
    z = jnp.einsum('bqhd,bkhd->bhqk', q, k, preferred_element_type=F32) * SB_SCALE
    valid = k_pos[None, :] < q_pos[:, None]
    log_beta = jax.nn.log_sigmoid(z)
    log_rem = jnp.where(valid, jax.nn.log_sigmoid(-z), 0.0)
    log_after = lax.cumsum(log_rem, axis=3, reverse=True) - log_rem
    a = jnp.where(valid, jnp.exp(log_beta + log_after), 0.0)
    o = jnp.einsum('bhqk,bkhd->bqhd', a.astype(v.dtype), v, preferred_element_type=F32)
    return o.astype(q.dtype)


def _mla_attend(q_lat, q_pe, ckv, kpe, q_pos, k_pos):
    s = (jnp.einsum('bqhr,bkr->bhqk', q_lat, ckv, preferred_element_type=F32)
         + jnp.einsum('bqhp,bkp->bhqk', q_pe, kpe, preferred_element_type=F32)) * MLA_SCALE
    s = jnp.where(k_pos[None, :] <= q_pos[:, None], s, -jnp.inf)
    p = jax.nn.softmax(s, axis=-1)
    o = jnp.einsum('bhqk,bkr->bqhr', p.astype(ckv.dtype), ckv, preferred_element_type=F32)
    return o.astype(q_lat.dtype)


def _rwkv7(rw, prev, P, j, wkv0):
    B, T, _ = rw.shape
    shifted = jnp.concatenate([prev, rw[:, :-1]], axis=1)
    xm = rw + (shifted - rw) * P['rw_mu'][j]
    r, wd, k, v, ad, gd = _split_cols(xm, (RW_WIDTH, RW_DECAY_LORA, RW_WIDTH, RW_WIDTH, RW_AAA_LORA, RW_GATE_LORA))
    w_log = -jax.nn.softplus(-(P['rw_w0'][j] + jnp.tanh(wd) @ P['rw_w2'][j])) - 0.5
    decay = jnp.exp(-jnp.exp(w_log.astype(F32)))
    a = jax.nn.sigmoid(P['rw_a0'][j] + ad @ P['rw_a2'][j])
    g = jax.nn.sigmoid(gd) @ P['rw_g2'][j]

    def heads(t):
        return t.astype(F32).reshape(B, T, RW_HEADS, RW_HEAD_DIM)

    kk = heads(k * P['rw_k_k'][j])
    kk = kk / jnp.maximum(jnp.sqrt(jnp.sum(kk * kk, axis=-1, keepdims=True)), 1e-12)
    k = k * (1.0 + (a - 1.0) * P['rw_k_a'][j])
    r_h, w_h, k_h, v_h, a_h = heads(r), heads(decay), heads(k), heads(v), heads(a)

    def step(S, inp):
        r_t, w_t, k_t, v_t, kk_t, a_t = inp
        s_kk = jnp.einsum('bhvk,bhk->bhv', S, kk_t)
        S = S * w_t[:, :, None, :] - s_kk[..., None] * (kk_t * a_t)[:, :, None, :] + v_t[..., None] * k_t[:, :, None, :]
        return S, jnp.einsum('bhvk,bhk->bhv', S, r_t)

    xs = tuple(jnp.swapaxes(t, 0, 1) for t in (r_h, w_h, k_h, v_h, kk, a_h))
    S, ys = lax.scan(step, wkv0.astype(F32), xs)
    y = jnp.swapaxes(ys, 0, 1)
    mu = jnp.mean(y, axis=-1, keepdims=True)
    var = jnp.mean(jnp.square(y - mu), axis=-1, keepdims=True)
    yn = ((y - mu) * lax.rsqrt(var + RW_GN_EPS)).reshape(B, T, RW_WIDTH) * P['rw_ln_w'][j].astype(F32) + P['rw_ln_b'][j].astype(F32)
    bonus = (jnp.sum(r_h * k_h * P['rw_r_k'][j].astype(F32), axis=-1, keepdims=True) * v_h).reshape(B, T, RW_WIDTH)
    out = (yn + bonus) * g.astype(F32)
    return out.astype(rw.dtype), S.astype(wkv0.dtype)


def _even_mixer(h, P, j, shift_prev, wkv0, past, q_pos):
    B, T, _ = h.shape
    w_in = P['ev_w_in'][j]
    proj = h @ w_in
    prev = (shift_prev @ w_in[:, :RW_PROJ])[:, None, :]
    rw_out, wkv = _rwkv7(proj[..., :RW_PROJ], prev, P, j, wkv0)
    q, k, v = (t.reshape(B, T, SB_HEADS, SB_HEAD_DIM) for t in _split_cols(proj[..., RW_PROJ:], (SB_WIDTH, SB_WIDTH, SB_WIDTH)))
    if past is None:
        k_all, v_all = k, v
    else:
        k_all = jnp.concatenate([past[0], k], axis=1)
        v_all = jnp.concatenate([past[1], v], axis=1)
    k_pos = jnp.arange(k_all.shape[1], dtype=jnp.int32)
    sb = _over_query_blocks(lambda qb, pb: _stick_breaking(qb, k_all, v_all, pb, k_pos), (q,), q_pos)
    out = jnp.concatenate([rw_out, sb.reshape(B, T, SB_WIDTH)], axis=-1) @ P['ev_w_out'][j]
    return out, h[:, -1], wkv, k, v


def _cplx_affine_combine(e1, e2):
    ar1, ai1, br1, bi1 = e1
    ar2, ai2, br2, bi2 = e2
    return (ar1 * ar2 - ai1 * ai2, ar1 * ai2 + ai1 * ar2,
            ar2 * br1 - ai2 * bi1 + br2, ar2 * bi1 + ai2 * br1 + bi2)


def _s5(u, P, j, s_re0, s_im0):
    B, T, _ = u.shape
    ug = u.astype(F32).reshape(B, T, S5_GROUPS, S5_GROUP_CH)
    lr = P['s5_lam_re'][j].astype(F32)
    li = P['s5_lam_im'][j].astype(F32)
    dt = jnp.exp(P['s5_log_step'][j].astype(F32))[:, None]
    mag = jnp.exp(lr * dt)
    ab_re, ab_im = mag * jnp.cos(li * dt), mag * jnp.sin(li * dt)
    den = lr * lr + li * li
    f_re = ((ab_re - 1.0) * lr + ab_im * li) / den
    f_im = (ab_im * lr - (ab_re - 1.0) * li) / den
    b_re, b_im = P['s5_b_re'][j].astype(F32), P['s5_b_im'][j].astype(F32)
    bb_re = f_re[..., None] * b_re - f_im[..., None] * b_im
    bb_im = f_re[..., None] * b_im + f_im[..., None] * b_re
    bu_re = jnp.einsum('btgc,gpc->btgp', ug, bb_re)
    bu_im = jnp.einsum('btgc,gpc->btgp', ug, bb_im)
    a_re = jnp.broadcast_to(ab_re, bu_re.shape)
    a_im = jnp.broadcast_to(ab_im, bu_im.shape)
    A_re, A_im, X_re, X_im = lax.associative_scan(_cplx_affine_combine, (a_re, a_im, bu_re, bu_im), axis=1)
    s_re, s_im = s_re0.astype(F32)[:, None], s_im0.astype(F32)[:, None]
    X_re, X_im = X_re + A_re * s_re - A_im * s_im, X_im + A_re * s_im + A_im * s_re
    c_re, c_im = P['s5_c_re'][j].astype(F32), P['s5_c_im'][j].astype(F32)
    y = jnp.einsum('btgp,gcp->btgc', X_re, c_re) - jnp.einsum('btgp,gcp->btgc', X_im, c_im)
    y = y.reshape(B, T, S5_WIDTH) + P['s5_d'][j].astype(F32) * u.astype(F32)
    y = jax.nn.gelu(y, approximate=False)
    out = y * jax.nn.sigmoid(y @ P['s5_w_glu'][j].astype(F32) + P['s5_b_glu'][j].astype(F32))
    return out.astype(u.dtype), X_re[:, -1].astype(s_re0.dtype), X_im[:, -1].astype(s_im0.dtype)


def _mla(qd, kvd, kr, P, j, past, q_pos):
    B, T, _ = qd.shape
    q = (_rmsnorm(qd, P['mla_q_norm_g'][j]) @ P['mla_w_q_up'][j]).reshape(B, T, MLA_HEADS, MLA_NOPE + MLA_ROPE)
    q_nope, q_pe = q[..., :MLA_NOPE], _rope(q[..., MLA_NOPE:], q_pos)
    ckv = _rmsnorm(kvd, P['mla_kv_norm_g'][j])
    kpe = _rope(kr, q_pos)
    q_lat = jnp.einsum('bthn,rhn->bthr', q_nope, P['mla_w_uk'][j])
    if past is None:
        ckv_all, kpe_all = ckv, kpe
    else:
        ckv_all = jnp.concatenate([past[0], ckv], axis=1)
        kpe_all = jnp.concatenate([past[1], kpe], axis=1)
    k_pos = jnp.arange(ckv_all.shape[1], dtype=jnp.int32)
    o_lat = _over_query_blocks(lambda ql, qp, pb: _mla_attend(ql, qp, ckv_all, kpe_all, pb, k_pos), (q_lat, q_pe), q_pos)
    o = jnp.einsum('bthr,rhv->bthv', o_lat, P['mla_w_uv'][j]).reshape(B, T, MLA_WIDTH)
    return o, ckv, kpe


def _odd_mixer(h, P, j, s_re0, s_im0, past, q_pos):
    proj = h @ P['od_w_in'][j]
    u, qd, kvd, kr = _split_cols(proj, (S5_WIDTH, MLA_Q_RANK, MLA_KV_RANK, MLA_ROPE))
    s5_out, s_re, s_im = _s5(u, P, j, s_re0, s_im0)
    mla_out, ckv, kpe = _mla(qd, kvd, kr, P, j, past, q_pos)
    out = jnp.concatenate([s5_out, mla_out], axis=-1) @ P['od_w_out'][j]
    return out, s_re, s_im, ckv, kpe


def _moe(h, P, l):
    B, T, D = h.shape
    x = h.reshape(B * T, D)
    logits = (x @ P['moe_w_router'][l] + P['moe_b_router'][l]).astype(F32)
    top_val, top_idx = lax.top_k(logits, TOP_K)
    gates = jax.nn.softmax(top_val, axis=-1)
    combine = jnp.einsum('nk,nke->ne', gates, jax.nn.one_hot(top_idx, N_EXPERTS, dtype=F32))

    def expert(acc, e):
        w_gu, b_gu, w_dn, b_dn, gate = e
        gu = x @ w_gu + b_gu
        glu = jnp.minimum(gu[..., 0::2], SWIGLU_LIMIT)
        lin = jnp.clip(gu[..., 1::2], -SWIGLU_LIMIT, SWIGLU_LIMIT)
        act = glu * jax.nn.sigmoid(SWIGLU_ALPHA * glu) * (lin + 1.0)
        return acc + gate[:, None] * (act @ w_dn + b_dn).astype(F32), None

    acc, _ = lax.scan(expert, jnp.zeros((B * T, D), F32),
                      (P['moe_w_gu'][l], P['moe_b_gu'][l], P['moe_w_down'][l], P['moe_b_down'][l], combine.T))
    return acc.astype(h.dtype).reshape(B, T, D)


def _trunk(x, c, q_pos, shift0, wkv0, s5re0, s5im0, paged, P):
    new = {name: [] for name in ('sb_k', 'sb_v', 'ckv', 'kpe', 'wkv', 'shift', 's5re', 's5im')}
    c_act = jax.nn.silu(c)
    for l in range(DEPTH):
        mod = (c_act @ P['w_mod'][l] + P['b_mod'][l])[:, None, :]
        sh1, sc1, g1, sh2, sc2, g2 = jnp.split(mod, 6, axis=-1)
        h = _rmsnorm(x, P['norm_mix_g'][l]) * (1.0 + sc1) + sh1
        j = l // 2
        if l % 2 == 0:
            past = None
            if paged is not None:
                past = (_gather_pages(paged['sb_k'], paged['page_table'], j),
                        _gather_pages(paged['sb_v'], paged['page_table'], j))
            mix, shift, wkv, k, v = _even_mixer(h, P, j, shift0[:, j], wkv0[:, j], past, q_pos)
            new['shift'].append(shift)
            new['wkv'].append(wkv)
            new['sb_k'].append(k)
            new['sb_v'].append(v)
        else:
            past = None
            if paged is not None:
                past = (_gather_pages(paged['ckv'], paged['page_table'], j),
                        _gather_pages(paged['kpe'], paged['page_table'], j))
            mix, s_re, s_im, ckv, kpe = _odd_mixer(h, P, j, s5re0[:, j], s5im0[:, j], past, q_pos)
            new['s5re'].append(s_re)
            new['s5im'].append(s_im)
            new['ckv'].append(ckv)
            new['kpe'].append(kpe)
        x = x + g1 * mix
        h = _rmsnorm(x, P['norm_ffn_g'][l]) * (1.0 + sc2) + sh2
        x = x + g2 * _moe(h, P, l)
    return _rmsnorm(x, P['norm_final_g']), {name: jnp.stack(v, axis=1) for name, v in new.items()}


def setup_inputs(seed: int = 0) -> dict:
    key = jax.random.key(seed)
    ks = iter(jax.random.split(key, 80))

    def nrm(shape, s=1.0):
        return jax.random.normal(next(ks), shape, F32) * s

    def unif(shape, lo, hi):
        return jax.random.uniform(next(ks), shape, F32, lo, hi)

    def gain(shape):
        return 1.0 + nrm(shape, 0.02)

    n_pages = PAST_LEN // PAGE_SIZE
    n_used = DEC_BATCH * n_pages
    n_pool = n_used + n_used // 4
    page_table = jax.random.permutation(next(ks), n_pool)[:n_used].reshape(DEC_BATCH, n_pages).astype(jnp.int32)
    return {
        'x_prompt': nrm((BATCH, SEQ, D_MODEL)),
        'x_sample': nrm((DEC_BATCH, DEC_SEQ, D_MODEL)),
        'cache_sb_k': nrm((n_pool, N_EVEN, PAGE_SIZE, SB_HEADS, SB_HEAD_DIM)),
        'cache_sb_v': nrm((n_pool, N_EVEN, PAGE_SIZE, SB_HEADS, SB_HEAD_DIM)),
        'cache_mla_ckv': nrm((n_pool, N_ODD, PAGE_SIZE, MLA_KV_RANK)),
        'cache_mla_kpe': nrm((n_pool, N_ODD, PAGE_SIZE, MLA_ROPE)),
        'state_rwkv_wkv': nrm((DEC_BATCH, N_EVEN, RW_HEADS, RW_HEAD_DIM, RW_HEAD_DIM), 0.3),
        'state_rwkv_shift': nrm((DEC_BATCH, N_EVEN, D_MODEL)),
        'state_s5_re': nrm((DEC_BATCH, N_ODD, S5_GROUPS, S5_STATE), 0.5),
        'state_s5_im': nrm((DEC_BATCH, N_ODD, S5_GROUPS, S5_STATE), 0.5),
        'page_table': page_table,
        'c_prompt': nrm((BATCH, D_MODEL)),
        'c_sample': nrm((DEC_BATCH, D_MODEL)),
        'w_mod': nrm((DEPTH, D_MODEL, 6 * D_MODEL), 0.5 * D_MODEL ** -0.5),
        'b_mod': nrm((DEPTH, 6 * D_MODEL), 0.02),
        'norm_mix_g': gain((DEPTH, D_MODEL)),
        'norm_ffn_g': gain((DEPTH, D_MODEL)),
        'norm_final_g': gain((D_MODEL,)),
        'ev_w_in': nrm((N_EVEN, D_MODEL, EVEN_IN), D_MODEL ** -0.5),
        'ev_w_out': nrm((N_EVEN, RW_WIDTH + SB_WIDTH, D_MODEL), (RW_WIDTH + SB_WIDTH) ** -0.5),
        'rw_mu': unif((N_EVEN, RW_PROJ), 0.0, 1.0),
        'rw_w0': unif((N_EVEN, RW_WIDTH), -6.0, 2.0),
        'rw_w2': nrm((N_EVEN, RW_DECAY_LORA, RW_WIDTH), 0.1),
        'rw_a0': nrm((N_EVEN, RW_WIDTH), 0.1),
        'rw_a2': nrm((N_EVEN, RW_AAA_LORA, RW_WIDTH), 0.5 * RW_AAA_LORA ** -0.5),
        'rw_g2': nrm((N_EVEN, RW_GATE_LORA, RW_WIDTH), RW_GATE_LORA ** -0.5),
        'rw_k_k': 0.85 + nrm((N_EVEN, RW_WIDTH), 0.02),
        'rw_k_a': gain((N_EVEN, RW_WIDTH)),
        'rw_r_k': nrm((N_EVEN, RW_HEADS, RW_HEAD_DIM), 0.1),
        'rw_ln_w': gain((N_EVEN, RW_WIDTH)),
        'rw_ln_b': nrm((N_EVEN, RW_WIDTH), 0.02),
        'od_w_in': nrm((N_ODD, D_MODEL, ODD_IN), D_MODEL ** -0.5),
        'od_w_out': nrm((N_ODD, S5_WIDTH + MLA_WIDTH, D_MODEL), (S5_WIDTH + MLA_WIDTH) ** -0.5),
        's5_lam_re': -0.5 + nrm((N_ODD, S5_GROUPS, S5_STATE), 0.01),
        's5_lam_im': jnp.pi * jnp.arange(S5_STATE, dtype=F32) + nrm((N_ODD, S5_GROUPS, S5_STATE), 0.01),
        's5_b_re': nrm((N_ODD, S5_GROUPS, S5_STATE, S5_GROUP_CH), (2 * S5_GROUP_CH) ** -0.5),
        's5_b_im': nrm((N_ODD, S5_GROUPS, S5_STATE, S5_GROUP_CH), (2 * S5_GROUP_CH) ** -0.5),
        's5_c_re': nrm((N_ODD, S5_GROUPS, S5_GROUP_CH, S5_STATE), (2 * S5_STATE) ** -0.5),
        's5_c_im': nrm((N_ODD, S5_GROUPS, S5_GROUP_CH, S5_STATE), (2 * S5_STATE) ** -0.5),
        's5_d': nrm((N_ODD, S5_WIDTH), 0.5),
        's5_log_step': unif((N_ODD, S5_GROUPS), math.log(1e-3), math.log(1e-1)),
        's5_w_glu': nrm((N_ODD, S5_WIDTH, S5_WIDTH), S5_WIDTH ** -0.5),
        's5_b_glu': nrm((N_ODD, S5_WIDTH), 0.02),
        'mla_q_norm_g': gain((N_ODD, MLA_Q_RANK)),
        'mla_w_q_up': nrm((N_ODD, MLA_Q_RANK, MLA_HEADS * (MLA_NOPE + MLA_ROPE)), MLA_Q_RANK ** -0.5),
        'mla_kv_norm_g': gain((N_ODD, MLA_KV_RANK)),
        'mla_w_uk': nrm((N_ODD, MLA_KV_RANK, MLA_HEADS, MLA_NOPE), MLA_KV_RANK ** -0.5),
        'mla_w_uv': nrm((N_ODD, MLA_KV_RANK, MLA_HEADS, MLA_V), MLA_KV_RANK ** -0.5),
        'moe_w_router': nrm((DEPTH, D_MODEL, N_EXPERTS), D_MODEL ** -0.5),
        'moe_b_router': nrm((DEPTH, N_EXPERTS), 0.01),
        'moe_w_gu': nrm((DEPTH, N_EXPERTS, D_MODEL, 2 * D_FF_EXPERT), D_MODEL ** -0.5),
        'moe_b_gu': nrm((DEPTH, N_EXPERTS, 2 * D_FF_EXPERT), 0.01),
        'moe_w_down': nrm((DEPTH, N_EXPERTS, D_FF_EXPERT, D_MODEL), D_FF_EXPERT ** -0.5),
        'moe_b_down': nrm((DEPTH, N_EXPERTS, D_MODEL), 0.01),
    }


def reference(x_prompt, x_sample, cache_sb_k, cache_sb_v, cache_mla_ckv, cache_mla_kpe, state_rwkv_wkv,
              state_rwkv_shift, state_s5_re, state_s5_im, page_table, c_prompt, c_sample, w_mod, b_mod,
              norm_mix_g, norm_ffn_g, norm_final_g, ev_w_in, ev_w_out, rw_mu, rw_w0, rw_w2, rw_a0, rw_a2, rw_g2,
              rw_k_k, rw_k_a, rw_r_k, rw_ln_w, rw_ln_b, od_w_in, od_w_out, s5_lam_re, s5_lam_im, s5_b_re, s5_b_im,
              s5_c_re, s5_c_im, s5_d, s5_log_step, s5_w_glu, s5_b_glu, mla_q_norm_g, mla_w_q_up, mla_kv_norm_g,
              mla_w_uk, mla_w_uv, moe_w_router, moe_b_router, moe_w_gu, moe_b_gu, moe_w_down, moe_b_down):
    P = dict(w_mod=w_mod, b_mod=b_mod, norm_mix_g=norm_mix_g, norm_ffn_g=norm_ffn_g, norm_final_g=norm_final_g,
             ev_w_in=ev_w_in, ev_w_out=ev_w_out, rw_mu=rw_mu, rw_w0=rw_w0, rw_w2=rw_w2, rw_a0=rw_a0, rw_a2=rw_a2,
             rw_g2=rw_g2, rw_k_k=rw_k_k, rw_k_a=rw_k_a, rw_r_k=rw_r_k, rw_ln_w=rw_ln_w, rw_ln_b=rw_ln_b,
             od_w_in=od_w_in, od_w_out=od_w_out, s5_lam_re=s5_lam_re, s5_lam_im=s5_lam_im, s5_b_re=s5_b_re,
             s5_b_im=s5_b_im, s5_c_re=s5_c_re, s5_c_im=s5_c_im, s5_d=s5_d, s5_log_step=s5_log_step,
             s5_w_glu=s5_w_glu, s5_b_glu=s5_b_glu, mla_q_norm_g=mla_q_norm_g, mla_w_q_up=mla_w_q_up,
             mla_kv_norm_g=mla_kv_norm_g, mla_w_uk=mla_w_uk, mla_w_uv=mla_w_uv, moe_w_router=moe_w_router,
             moe_b_router=moe_b_router, moe_w_gu=moe_w_gu, moe_b_gu=moe_b_gu, moe_w_down=moe_w_down,
             moe_b_down=moe_b_down)
    dt = x_prompt.dtype
    bp, sp = x_prompt.shape[0], x_prompt.shape[1]
    y_prompt, pn = _trunk(
        x_prompt, c_prompt, jnp.arange(sp, dtype=jnp.int32),
        jnp.zeros((bp, N_EVEN, D_MODEL), dt),
        jnp.zeros((bp, N_EVEN, RW_HEADS, RW_HEAD_DIM, RW_HEAD_DIM), dt),
        jnp.zeros((bp, N_ODD, S5_GROUPS, S5_STATE), dt),
        jnp.zeros((bp, N_ODD, S5_GROUPS, S5_STATE), dt),
        None, P)
    past_len = page_table.shape[1] * cache_sb_k.shape[2]
    paged = dict(page_table=page_table, sb_k=cache_sb_k, sb_v=cache_sb_v, ckv=cache_mla_ckv, kpe=cache_mla_kpe)
    y_sample, sn = _trunk(
        x_sample, c_sample, past_len + jnp.arange(x_sample.shape[1], dtype=jnp.int32),
        state_rwkv_shift, state_rwkv_wkv, state_s5_re, state_s5_im, paged, P)
    return (y_prompt, y_sample,
            pn['sb_k'], pn['sb_v'], pn['ckv'], pn['kpe'], pn['wkv'], pn['shift'], pn['s5re'], pn['s5im'],
            sn['sb_k'], sn['sb_v'], sn['ckv'], sn['kpe'], sn['wkv'], sn['shift'], sn['s5re'], sn['s5im'])
```

```python
import functools
import math

import numpy as np
import jax
import jax.numpy as jnp
from jax import lax
from jax.experimental import pallas as pl
from jax.experimental.pallas import tpu as pltpu

F32 = jnp.float32
BF16 = jnp.bfloat16
HIGHEST = lax.Precision.HIGHEST

D_MODEL = 1024
DEPTH = 4
PAGE_SIZE = 128
NORM_EPS = 1e-6

RW_HEADS = 8
RW_HEAD_DIM = 64
RW_WIDTH = RW_HEADS * RW_HEAD_DIM
RW_DECAY_LORA = 64
RW_AAA_LORA = 64
RW_GATE_LORA = 128
RW_PROJ = 3 * RW_WIDTH + RW_DECAY_LORA + RW_AAA_LORA + RW_GATE_LORA
RW_GN_EPS = 64e-5

SB_HEADS = 8
SB_HEAD_DIM = 64
SB_WIDTH = SB_HEADS * SB_HEAD_DIM
SB_SCALE = SB_HEAD_DIM ** -0.5

S5_GROUPS = 32
S5_GROUP_CH = 16
S5_WIDTH = S5_GROUPS * S5_GROUP_CH
S5_STATE = 64
S5_FLAT = S5_GROUPS * S5_STATE

MLA_HEADS = 8
MLA_Q_RANK = 384
MLA_KV_RANK = 256
MLA_NOPE = 64
MLA_ROPE = 32
MLA_V = 64
MLA_WIDTH = MLA_HEADS * MLA_V
MLA_SCALE = (MLA_NOPE + MLA_ROPE) ** -0.5
ROPE_THETA = 10000.0
ROPE_HALF = MLA_ROPE // 2
KR_PAD = 128

N_EXPERTS = 32
TOP_K = 4
D_FF_EXPERT = 512
SWIGLU_LIMIT = 7.0
SWIGLU_ALPHA = 1.702

VMEM_LIMIT_BYTES = 56 * 1024 * 1024
QBLK = 128
MLA_KBLK = 256
PAGES_PER_STEP = 8


def _cparams(*sem):
    return pltpu.CompilerParams(dimension_semantics=sem, vmem_limit_bytes=VMEM_LIMIT_BYTES)


def _blocking(S, T, rows):
    if T >= rows:
        assert T % rows == 0
        return 1, rows
    sb = max(1, min(S, rows // T))
    assert S % sb == 0
    return sb, T


def _softplus(x):
    return jnp.maximum(x, 0.0) + jnp.log1p(jnp.exp(-jnp.abs(x)))


def _dot(a, b):
    return jnp.dot(a, b, preferred_element_type=F32)


def _dot_nt(a, b):
    return lax.dot_general(a, b, (((1,), (1,)), ((), ())), preferred_element_type=F32)


def _split_hi_lo(x):
    hi = x.astype(BF16)
    lo = (x - hi.astype(F32)).astype(BF16)
    return hi, lo


def _linear_body(*refs, n_out, splits, has_norm, has_mod, silu, has_x2, has_b, has_res, sb, tb):
    outs = refs[len(refs) - n_out:]
    it = iter(refs[:len(refs) - n_out])
    x = next(it)[...]
    if has_norm:
        g_ref = next(it)
        x = x * lax.rsqrt(jnp.mean(x * x, axis=-1, keepdims=True) + NORM_EPS) * g_ref[...]
    if has_mod:
        sc_ref, sh_ref = next(it), next(it)
        x = x * (1.0 + sc_ref[...]) + sh_ref[...]
    if silu:
        x = x * jax.nn.sigmoid(x)
    w_ref = next(it)
    rows = sb * tb
    y = _dot(x.reshape(rows, x.shape[-1]).astype(BF16), w_ref[...])
    if has_x2:
        x2_ref, w2_ref = next(it), next(it)
        x2 = x2_ref[...]
        y = y + _dot(x2.reshape(rows, x2.shape[-1]).astype(BF16), w2_ref[...])
    if has_b:
        y = y + next(it)[...]
    if has_res:
        res_ref, gate_ref = next(it), next(it)
    off = 0
    for o_ref, n in zip(outs, splits):
        piece = y[:, off:off + n].reshape(sb, tb, n)
        if has_res:
            piece = res_ref[...] + gate_ref[...] * piece
        o_ref[...] = piece
        off += n


def _linear(x, w, *, b=None, norm_g=None, scale=None, shift=None, silu=False, x2=None, w2=None,
            res=None, gate=None, splits=None, rows=512, tn=None):
    S, T, K = x.shape
    N = w.shape[1]
    splits = list(splits) if splits is not None else [N]
    tn = N if tn is None else tn
    assert N % tn == 0 and (tn == N or len(splits) == 1)
    assert res is None or len(splits) == 1
    sb, tb = _blocking(S, T, rows)
    grid = (S // sb, T // tb, N // tn)

    def row_spec(width):
        return pl.BlockSpec((sb, tb, width), lambda i, j, n: (i, j, 0))

    def seq_spec(width):
        return pl.BlockSpec((sb, 1, width), lambda i, j, n: (i, 0, 0))

    args, specs = [x], [row_spec(K)]
    if norm_g is not None:
        args.append(norm_g.reshape(1, K))
        specs.append(pl.BlockSpec((1, K), lambda i, j, n: (0, 0)))
    if scale is not None:
        args += [scale, shift]
        specs += [seq_spec(K), seq_spec(K)]
    args.append(w)
    specs.append(pl.BlockSpec((K, tn), lambda i, j, n: (0, n)))
    if x2 is not None:
        K2 = x2.shape[-1]
        args += [x2, w2]
        specs += [row_spec(K2), pl.BlockSpec((K2, tn), lambda i, j, n: (0, n))]
    if b is not None:
        args.append(b.reshape(1, N))
        specs.append(pl.BlockSpec((1, tn), lambda i, j, n: (0, n)))
    if res is not None:
        args += [res, gate]
        specs += [pl.BlockSpec((sb, tb, tn), lambda i, j, n: (i, j, n)),
                  pl.BlockSpec((sb, 1, tn), lambda i, j, n: (i, 0, n))]
    if tn == N:
        out_specs = [row_spec(n) for n in splits]
        blk_splits = splits
    else:
        out_specs = [pl.BlockSpec((sb, tb, tn), lambda i, j, n: (i, j, n))]
        blk_splits = [tn]
    body = functools.partial(
        _linear_body, n_out=len(splits), splits=blk_splits, has_norm=norm_g is not None,
        has_mod=scale is not None, silu=silu, has_x2=x2 is not None, has_b=b is not None,
        has_res=res is not None, sb=sb, tb=tb)
    return pl.pallas_call(
        body,
        out_shape=[jax.ShapeDtypeStruct((S, T, n), F32) for n in splits],
        grid=grid, in_specs=specs, out_specs=out_specs,
        compiler_params=_cparams("parallel", "parallel", "arbitrary"),
    )(*args)


def _normmod_body(*refs, has_mod):
    it = iter(refs)
    x = next(it)[...]
    g_ref = next(it)
    y = x * lax.rsqrt(jnp.mean(x * x, axis=-1, keepdims=True) + NORM_EPS) * g_ref[...]
    if has_mod:
        sc_ref, sh_ref = next(it), next(it)
        y = y * (1.0 + sc_ref[...]) + sh_ref[...]
    next(it)[...] = y


def _normmod(x, g, scale=None, shift=None, rows=512):
    S, T, K = x.shape
    sb, tb = _blocking(S, T, rows)
    args = [x, g.reshape(1, K)]
    specs = [pl.BlockSpec((sb, tb, K), lambda i, j: (i, j, 0)), pl.BlockSpec((1, K), lambda i, j: (0, 0))]
    if scale is not None:
        args += [scale, shift]
        specs += [pl.BlockSpec((sb, 1, K), lambda i, j: (i, 0, 0))] * 2
    return pl.pallas_call(
        functools.partial(_normmod_body, has_mod=scale is not None),
        out_shape=jax.ShapeDtypeStruct((S, T, K), F32),
        grid=(S // sb, T // tb), in_specs=specs,
        out_specs=pl.BlockSpec((sb, tb, K), lambda i, j: (i, j, 0)),
        compiler_params=_cparams("parallel", "parallel"),
    )(*args)


RW_COLS = RW_PROJ


def _rwkv_body(*refs, has_state, sb, tb, chunk):
    it = iter(refs)
    rw_ref, prev_ref = next(it), next(it)
    wkv0_ref = next(it) if has_state else None
    (mu_ref, w0_ref, w2_ref, a0_ref, a2_ref, g2_ref, kk_w_ref, ka_ref, rk_ref, lnw_ref, lnb_ref,
     ones_ref) = (next(it) for _ in range(12))
    out_ref, wkv_ref = next(it), next(it)
    carry_s, state_s, r_s, w_s, k_s, v_s, kk_s, b_s, y_s = (next(it) for _ in range(9))
    H, N, W = RW_HEADS, RW_HEAD_DIM, RW_WIDTH
    rows = sb * tb
    tj = pl.program_id(1)

    @pl.when(tj == 0)
    def _():
        carry_s[...] = prev_ref[...]
        for h in range(H):
            state_s[h] = wkv0_ref[:, h] if has_state else jnp.zeros((sb, N, N), F32)

    x = rw_ref[...]
    rolled = pltpu.roll(x.reshape(rows, RW_COLS), 1, 0).reshape(sb, tb, RW_COLS)
    tpos = lax.broadcasted_iota(jnp.int32, (sb, tb, 1), 1)
    shifted = jnp.where(tpos == 0, carry_s[...], rolled)
    carry_s[...] = x[:, tb - 1:tb, :]
    xm = (x + (shifted - x) * mu_ref[...]).reshape(rows, RW_COLS)
    r, k, v = xm[:, 0:W], xm[:, W:2 * W], xm[:, 2 * W:3 * W]
    o = 3 * W
    wd = xm[:, o:o + RW_DECAY_LORA]
    ad = xm[:, o + RW_DECAY_LORA:o + RW_DECAY_LORA + RW_AAA_LORA]
    gd = xm[:, o + RW_DECAY_LORA + RW_AAA_LORA:RW_COLS]

    w_log = -_softplus(-(w0_ref[...] + _dot(jnp.tanh(wd).astype(BF16), w2_ref[...]))) - 0.5
    decay = jnp.exp(-jnp.exp(w_log))
    a = jax.nn.sigmoid(a0_ref[...] + _dot(ad.astype(BF16), a2_ref[...]))
    g = _dot(jax.nn.sigmoid(gd).astype(BF16), g2_ref[...])
    ones_blk = ones_ref[...]
    kk = k * kk_w_ref[...]
    kk_sq = jnp.dot(kk * kk, ones_blk, precision=HIGHEST, preferred_element_type=F32)
    kk = kk / jnp.maximum(jnp.sqrt(kk_sq), 1e-12)
    k = k * (1.0 + (a - 1.0) * ka_ref[...])
    bb = kk * a
    for h in range(H):
        hs = slice(h * N, (h + 1) * N)
        for dst, src in ((r_s, r), (w_s, decay), (k_s, k), (v_s, v), (kk_s, kk), (b_s, bb)):
            dst[h] = src[:, hs].reshape(sb, tb, N)

    eye = (lax.broadcasted_iota(jnp.int32, (N, N), 0) == lax.broadcasted_iota(jnp.int32, (N, N), 1)).astype(F32)

    def chunk_body(c, carry):
        t0 = pl.multiple_of(c * chunk, chunk)
        y_rows = [[] for _ in range(H)]
        for i in range(chunk):
            ts = pl.ds(t0 + i, 1)
            for h in range(H):
                st = state_s[h]
                kk_t, w_t, b_t = kk_s[h, :, ts, :], w_s[h, :, ts, :], b_s[h, :, ts, :]
                k_t, v_t, r_t = k_s[h, :, ts, :], v_s[h, :, ts, :], r_s[h, :, ts, :]
                s_kk = jnp.sum(st * kk_t, axis=-1, keepdims=True)
                v_col = jnp.sum(eye * v_t, axis=-1, keepdims=True)
                st = st * w_t - s_kk * b_t + v_col * k_t
                y_col = jnp.sum(st * r_t, axis=-1, keepdims=True)
                state_s[h] = st
                y_rows[h].append(jnp.sum(y_col * eye, axis=-2, keepdims=True))
        for h in range(H):
            y_s[:, pl.ds(t0, chunk), h * N:(h + 1) * N] = jnp.concatenate(y_rows[h], axis=1)
        return carry

    lax.fori_loop(0, tb // chunk, chunk_body, 0)

    y = y_s[...].reshape(rows, W)
    avg_blk = ones_blk * (1.0 / N)
    mean = jnp.dot(y, avg_blk, precision=HIGHEST, preferred_element_type=F32)
    dev = y - mean
    var = jnp.dot(dev * dev, avg_blk, precision=HIGHEST, preferred_element_type=F32)
    yn = dev * lax.rsqrt(var + RW_GN_EPS) * lnw_ref[...] + lnb_ref[...]
    bonus = jnp.dot(r * k * rk_ref[...], ones_blk, precision=HIGHEST, preferred_element_type=F32) * v
    out_ref[...] = ((yn + bonus) * g).reshape(sb, tb, W)

    @pl.when(tj == pl.num_programs(1) - 1)
    def _():
        for h in range(H):
            wkv_ref[:, h] = state_s[h]


def _rwkv(rw, prev, wkv0, prm, *, sb, tb):
    S, T, _ = rw.shape
    H, N, W = RW_HEADS, RW_HEAD_DIM, RW_WIDTH
    chunk = min(tb, 16)
    has_state = wkv0 is not None
    args = [rw, prev]
    specs = [pl.BlockSpec((sb, tb, RW_COLS), lambda i, j: (i, j, 0)),
             pl.BlockSpec((sb, 1, RW_COLS), lambda i, j: (i, 0, 0))]
    if has_state:
        args.append(wkv0)
        specs.append(pl.BlockSpec((sb, H, N, N), lambda i, j: (i, 0, 0, 0)))
    for name in ("mu", "w0", "w2", "a0", "a2", "g2", "k_k", "k_a", "r_k", "ln_w", "ln_b", "ones"):
        arr = prm[name]
        args.append(arr)
        specs.append(pl.BlockSpec(arr.shape, lambda i, j: (0, 0)))
    head_scratch = pltpu.VMEM((H, sb, tb, N), F32)
    return pl.pallas_call(
        functools.partial(_rwkv_body, has_state=has_state, sb=sb, tb=tb, chunk=chunk),
        out_shape=[jax.ShapeDtypeStruct((S, T, W), F32), jax.ShapeDtypeStruct((S, H, N, N), F32)],
        grid=(S // sb, T // tb), in_specs=specs,
        out_specs=[pl.BlockSpec((sb, tb, W), lambda i, j: (i, j, 0)),
                   pl.BlockSpec((sb, H, N, N), lambda i, j: (i, 0, 0, 0))],
        scratch_shapes=[pltpu.VMEM((sb, 1, RW_COLS), F32), pltpu.VMEM((H, sb, N, N), F32)]
        + [head_scratch] * 6 + [pltpu.VMEM((sb, tb, W), F32)],
        compiler_params=_cparams("parallel", "arbitrary"),
    )(*args)


def _sb_tile(qh, kh, vh, valid, tri, run):
    z = _dot_nt(qh, kh)
    log_beta = -_softplus(-z)
    log_rem = log_beta - z
    if valid is not None:
        log_rem = jnp.where(valid, log_rem, 0.0)
    hi, lo = _split_hi_lo(log_rem)
    after = _dot(hi, tri) + _dot(lo, tri)
    a = jnp.exp(log_beta + after + run)
    if valid is not None:
        a = jnp.where(valid, a, 0.0)
    return _dot(a.astype(BF16), vh), run + jnp.sum(log_rem, axis=-1, keepdims=True)


def _later_key_matrix(n):
    j = lax.broadcasted_iota(jnp.int32, (n, n), 0)
    s = lax.broadcasted_iota(jnp.int32, (n, n), 1)
    return (j > s).astype(BF16)


def _sb_prompt_body(q_ref, k_ref, v_ref, o_ref):
    qi = pl.program_id(1)
    D = SB_HEAD_DIM
    row = lax.broadcasted_iota(jnp.int32, (QBLK, QBLK), 0)
    col = lax.broadcasted_iota(jnp.int32, (QBLK, QBLK), 1)
    tri = _later_key_matrix(QBLK)
    for h in range(SB_HEADS):
        hs = slice(h * D, (h + 1) * D)
        qh = (q_ref[0, :, hs] * SB_SCALE).astype(BF16)

        def kb_body(i, carry, hs=hs, qh=qh):
            acc, run = carry
            off = pl.multiple_of((qi - i) * QBLK, QBLK)
            kh = k_ref[0, pl.ds(off, QBLK), hs].astype(BF16)
            vh = v_ref[0, pl.ds(off, QBLK), hs].astype(BF16)
            valid = (col - row) < i * QBLK
            o, run = _sb_tile(qh, kh, vh, valid, tri, run)
            return acc + o, run

        acc, _ = lax.fori_loop(0, qi + 1, kb_body,
                               (jnp.zeros((QBLK, D), F32), jnp.zeros((QBLK, 1), F32)))
        o_ref[0, :, hs] = acc


def _sb_prompt(q, k, v):
    B, T, W = q.shape
    return pl.pallas_call(
        _sb_prompt_body,
        out_shape=jax.ShapeDtypeStruct((B, T, W), F32),
        grid=(B, T // QBLK),
        in_specs=[pl.BlockSpec((1, QBLK, W), lambda b, i: (b, i, 0)),
                  pl.BlockSpec((1, T, W), lambda b, i: (b, 0, 0)),
                  pl.BlockSpec((1, T, W), lambda b, i: (b, 0, 0))],
        out_specs=pl.BlockSpec((1, QBLK, W), lambda b, i: (b, i, 0)),
        compiler_params=_cparams("parallel", "arbitrary"),
    )(q, k, v)


def _sb_paged_body(pt_ref, q_ref, kn_ref, vn_ref, *refs, n_pg):
    k_pages, v_pages = refs[:n_pg], refs[n_pg:2 * n_pg]
    o_ref, acc_s, run_s = refs[2 * n_pg:]
    g = pl.program_id(1)
    D, T = SB_HEAD_DIM, q_ref.shape[1]
    tri = _later_key_matrix(PAGE_SIZE)

    @pl.when(g == 0)
    def _():
        row = lax.broadcasted_iota(jnp.int32, (T, PAGE_SIZE), 0)
        col = lax.broadcasted_iota(jnp.int32, (T, PAGE_SIZE), 1)
        valid = col < row
        for h in range(SB_HEADS):
            hs = slice(h * D, (h + 1) * D)
            qh = (q_ref[0, :, hs] * SB_SCALE).astype(BF16)
            o, run = _sb_tile(qh, kn_ref[0, :, hs].astype(BF16), vn_ref[0, :, hs].astype(BF16),
                              valid, tri, jnp.zeros((T, 1), F32))
            acc_s[h] = o
            run_s[h] = run

    for p in range(n_pg):
        for h in range(SB_HEADS):
            hs = slice(h * D, (h + 1) * D)
            qh = (q_ref[0, :, hs] * SB_SCALE).astype(BF16)
            o, run = _sb_tile(qh, k_pages[p][:, h, :].astype(BF16), v_pages[p][:, h, :].astype(BF16),
                              None, tri, run_s[h])
            acc_s[h] = acc_s[h] + o
            run_s[h] = run

    @pl.when(g == pl.num_programs(1) - 1)
    def _():
        for h in range(SB_HEADS):
            o_ref[0, :, h * D:(h + 1) * D] = acc_s[h]


def _sb_paged(q, k_new, v_new, cache_k, cache_v, page_table, j):
    B, T, W = q.shape
    n_pages = page_table.shape[1]
    n_pg = PAGES_PER_STEP
    assert n_pages % n_pg == 0

    def page_spec(p):
        return pl.BlockSpec((None, None, PAGE_SIZE, SB_HEADS, SB_HEAD_DIM),
                            lambda b, g, pt: (pt[b, n_pages - 1 - (g * n_pg + p)], j, 0, 0, 0))

    new_spec = pl.BlockSpec((1, PAGE_SIZE, W), lambda b, g, pt: (b, 0, 0))
    row_spec = pl.BlockSpec((1, T, W), lambda b, g, pt: (b, 0, 0))
    return pl.pallas_call(
        functools.partial(_sb_paged_body, n_pg=n_pg),
        out_shape=jax.ShapeDtypeStruct((B, T, W), F32),
        grid_spec=pltpu.PrefetchScalarGridSpec(
            num_scalar_prefetch=1, grid=(B, n_pages // n_pg),
            in_specs=[row_spec, new_spec, new_spec] + [page_spec(p) for p in range(n_pg)] * 2,
            out_specs=row_spec,
            scratch_shapes=[pltpu.VMEM((SB_HEADS, T, SB_HEAD_DIM), F32), pltpu.VMEM((SB_HEADS, T, 1), F32)]),
        compiler_params=_cparams("parallel", "arbitrary"),
    )(page_table, q, k_new, v_new, *([cache_k] * n_pg), *([cache_v] * n_pg))


def _rope_lanes(x, cos, sin):
    W = x.shape[-1]
    lane = lax.broadcasted_iota(jnp.int32, x.shape, 1)
    first = (lane % MLA_ROPE) < ROPE_HALF
    from_right = pltpu.roll(x, W - ROPE_HALF, 1)
    from_left = pltpu.roll(x, ROPE_HALF, 1)
    return x * cos + jnp.where(first, -from_right, from_left) * sin


def _mla_prep_body(qd_ref, kvd_ref, kr_ref, cq_ref, sq_ref, ck_ref, sk_ref, qg_ref, wq_ref, kvg_ref, wuk_ref,
                   ql_ref, qp_ref, ckv_ref, kpe_ref, *, sb, tb):
    rows = sb * tb
    nope_w = MLA_HEADS * MLA_NOPE

    def table(ref):
        t = ref[...]
        return jnp.broadcast_to(t[None], (sb,) + t.shape).reshape(rows, t.shape[-1])

    qd = qd_ref[...]
    qn = qd * lax.rsqrt(jnp.mean(qd * qd, axis=-1, keepdims=True) + NORM_EPS) * qg_ref[...]
    q = _dot(qn.reshape(rows, MLA_Q_RANK).astype(BF16), wq_ref[...])
    q_pe = _rope_lanes(q[:, nope_w:], table(cq_ref), table(sq_ref))
    qp_ref[...] = q_pe.reshape(sb, tb, MLA_HEADS * MLA_ROPE)
    for h in range(MLA_HEADS):
        q_lat = _dot(q[:, h * MLA_NOPE:(h + 1) * MLA_NOPE].astype(BF16), wuk_ref[h])
        ql_ref[:, :, h * MLA_KV_RANK:(h + 1) * MLA_KV_RANK] = q_lat.reshape(sb, tb, MLA_KV_RANK)
    kvd = kvd_ref[...]
    ckv_ref[...] = kvd * lax.rsqrt(jnp.mean(kvd * kvd, axis=-1, keepdims=True) + NORM_EPS) * kvg_ref[...]
    kpe = _rope_lanes(kr_ref[...].reshape(rows, KR_PAD), table(ck_ref), table(sk_ref))
    kpe_ref[...] = kpe.reshape(sb, tb, KR_PAD)


def _mla_prep(qd, kvd, kr, tables, prm, rows=256):
    S, T, _ = qd.shape
    sb, tb = _blocking(S, T, rows)
    cos_q, sin_q, cos_k, sin_k = tables

    def row_spec(width):
        return pl.BlockSpec((sb, tb, width), lambda i, j: (i, j, 0))

    def tab_spec(width):
        return pl.BlockSpec((tb, width), lambda i, j: (j, 0))

    def full_spec(arr):
        return pl.BlockSpec(arr.shape, lambda i, j: (0,) * arr.ndim)

    widths = (MLA_HEADS * MLA_KV_RANK, MLA_HEADS * MLA_ROPE, MLA_KV_RANK, KR_PAD)
    return pl.pallas_call(
        functools.partial(_mla_prep_body, sb=sb, tb=tb),
        out_shape=[jax.ShapeDtypeStruct((S, T, n), F32) for n in widths],
        grid=(S // sb, T // tb),
        in_specs=[row_spec(MLA_Q_RANK), row_spec(MLA_KV_RANK), row_spec(KR_PAD),
                  tab_spec(cos_q.shape[1]), tab_spec(cos_q.shape[1]), tab_spec(KR_PAD), tab_spec(KR_PAD),
                  full_spec(prm["q_g"]), full_spec(prm["w_q"]), full_spec(prm["kv_g"]), full_spec(prm["w_uk"])],
        out_specs=[row_spec(n) for n in widths],
        compiler_params=_cparams("parallel", "parallel"),
    )(qd, kvd, kr, cos_q, sin_q, cos_k, sin_k, prm["q_g"], prm["w_q"], prm["kv_g"], prm["w_uk"])


def _softmax_step(s, c, m_s, l_s, acc_s):
    m_old = m_s[...]
    m_new = jnp.maximum(m_old, jnp.max(s, axis=-1, keepdims=True))
    alpha = jnp.exp(m_old - m_new)
    p = jnp.exp(s - m_new)
    l_s[...] = alpha * l_s[...] + jnp.sum(p, axis=-1, keepdims=True)
    acc_s[...] = alpha * acc_s[...] + _dot(p.astype(BF16), c)
    m_s[...] = m_new


def _stack_heads(ql_ref, qp_ref, ql_s, qp_s, T):
    for h in range(MLA_HEADS):
        ql_s[h * T:(h + 1) * T, :] = ql_ref[0, :, h * MLA_KV_RANK:(h + 1) * MLA_KV_RANK].astype(ql_s.dtype)
        qp_s[h * T:(h + 1) * T, :] = qp_ref[0, :, h * MLA_ROPE:(h + 1) * MLA_ROPE].astype(qp_s.dtype)


def _mla_finish(o_ref, wuv_ref, l_s, acc_s, T):
    o_lat = acc_s[...] / l_s[...]
    for h in range(MLA_HEADS):
        o_ref[0, :, h * MLA_V:(h + 1) * MLA_V] = _dot(o_lat[h * T:(h + 1) * T].astype(BF16), wuv_ref[h])


def _mla_prompt_body(ql_ref, qp_ref, ckv_ref, kpe_ref, wuv_ref, o_ref, ql_s, qp_s, m_s, l_s, acc_s):
    qi = pl.program_id(1)
    rows = MLA_HEADS * QBLK
    _stack_heads(ql_ref, qp_ref, ql_s, qp_s, QBLK)
    m_s[...] = jnp.full(m_s.shape, -jnp.inf, F32)
    l_s[...] = jnp.zeros(l_s.shape, F32)
    acc_s[...] = jnp.zeros(acc_s.shape, F32)
    q_pos = qi * QBLK + lax.broadcasted_iota(jnp.int32, (rows, MLA_KBLK), 0) % QBLK
    col = lax.broadcasted_iota(jnp.int32, (rows, MLA_KBLK), 1)

    def kb_body(kb, carry):
        off = pl.multiple_of(kb * MLA_KBLK, MLA_KBLK)
        c = ckv_ref[0, pl.ds(off, MLA_KBLK), :].astype(BF16)
        pe = kpe_ref[0, pl.ds(off, MLA_KBLK), 0:MLA_ROPE].astype(BF16)
        s = (_dot_nt(ql_s[...], c) + _dot_nt(qp_s[...], pe)) * MLA_SCALE
        s = jnp.where(off + col <= q_pos, s, -jnp.inf)
        _softmax_step(s, c, m_s, l_s, acc_s)
        return carry

    n_kb = (qi * QBLK + QBLK + MLA_KBLK - 1) // MLA_KBLK
    lax.fori_loop(0, n_kb, kb_body, 0)
    _mla_finish(o_ref, wuv_ref, l_s, acc_s, QBLK)


def _mla_prompt(q_lat, q_pe, ckv, kpe, w_uv):
    B, T, _ = q_lat.shape
    assert T % MLA_KBLK == 0
    rows = MLA_HEADS * QBLK
    return pl.pallas_call(
        _mla_prompt_body,
        out_shape=jax.ShapeDtypeStruct((B, T, MLA_WIDTH), F32),
        grid=(B, T // QBLK),
        in_specs=[pl.BlockSpec((1, QBLK, q_lat.shape[2]), lambda b, i: (b, i, 0)),
                  pl.BlockSpec((1, QBLK, q_pe.shape[2]), lambda b, i: (b, i, 0)),
                  pl.BlockSpec((1, T, MLA_KV_RANK), lambda b, i: (b, 0, 0)),
                  pl.BlockSpec((1, T, KR_PAD), lambda b, i: (b, 0, 0)),
                  pl.BlockSpec(w_uv.shape, lambda b, i: (0, 0, 0))],
        out_specs=pl.BlockSpec((1, QBLK, MLA_WIDTH), lambda b, i: (b, i, 0)),
        scratch_shapes=[pltpu.VMEM((rows, MLA_KV_RANK), BF16), pltpu.VMEM((rows, MLA_ROPE), BF16),
                        pltpu.VMEM((rows, 1), F32), pltpu.VMEM((rows, 1), F32),
                        pltpu.VMEM((rows, MLA_KV_RANK), F32)],
        compiler_params=_cparams("parallel", "arbitrary"),
    )(q_lat, q_pe, ckv, kpe, w_uv)


def _mla_paged_body(pt_ref, ql_ref, qp_ref, cn_ref, pn_ref, *refs, n_pg):
    c_pages, p_pages = refs[:n_pg], refs[n_pg:2 * n_pg]
    wuv_ref, o_ref, ql_s, qp_s, m_s, l_s, acc_s = refs[2 * n_pg:]
    g = pl.program_id(1)
    T = ql_ref.shape[1]
    rows = MLA_HEADS * T

    @pl.when(g == 0)
    def _():
        _stack_heads(ql_ref, qp_ref, ql_s, qp_s, T)
        m_s[...] = jnp.full(m_s.shape, -jnp.inf, F32)
        l_s[...] = jnp.zeros(l_s.shape, F32)
        acc_s[...] = jnp.zeros(acc_s.shape, F32)
        t_pos = lax.broadcasted_iota(jnp.int32, (rows, PAGE_SIZE), 0) % T
        col = lax.broadcasted_iota(jnp.int32, (rows, PAGE_SIZE), 1)
        c = cn_ref[0].astype(BF16)
        pe = pn_ref[0, :, 0:MLA_ROPE].astype(BF16)
        s = (_dot_nt(ql_s[...].astype(BF16), c) + _dot_nt(qp_s[...].astype(BF16), pe)) * MLA_SCALE
        _softmax_step(jnp.where(col <= t_pos, s, -jnp.inf), c, m_s, l_s, acc_s)

    q_lat, q_pe = ql_s[...].astype(BF16), qp_s[...].astype(BF16)
    for p in range(n_pg):
        c = c_pages[p][...].astype(BF16)
        pe = p_pages[p][...].astype(BF16)
        s = (_dot_nt(q_lat, c) + _dot_nt(q_pe, pe)) * MLA_SCALE
        _softmax_step(s, c, m_s, l_s, acc_s)

    @pl.when(g == pl.num_programs(1) - 1)
    def _():
        _mla_finish(o_ref, wuv_ref, l_s, acc_s, T)


def _mla_paged(q_lat, q_pe, ckv_new, kpe_new, cache_ckv, cache_kpe, page_table, j, w_uv):
    B, T, _ = q_lat.shape
    n_pages = page_table.shape[1]
    n_pg = PAGES_PER_STEP
    assert n_pages % n_pg == 0
    rows = MLA_HEADS * T

    def page_spec(width, p):
        return pl.BlockSpec((None, None, PAGE_SIZE, width),
                            lambda b, g, pt: (pt[b, g * n_pg + p], j, 0, 0))

    def seq_spec(t, width):
        return pl.BlockSpec((1, t, width), lambda b, g, pt: (b, 0, 0))

    return pl.pallas_call(
        functools.partial(_mla_paged_body, n_pg=n_pg),
        out_shape=jax.ShapeDtypeStruct((B, T, MLA_WIDTH), F32),
        grid_spec=pltpu.PrefetchScalarGridSpec(
            num_scalar_prefetch=1, grid=(B, n_pages // n_pg),
            in_specs=[seq_spec(T, q_lat.shape[2]), seq_spec(T, q_pe.shape[2]),
                      seq_spec(PAGE_SIZE, MLA_KV_RANK), seq_spec(PAGE_SIZE, KR_PAD)]
            + [page_spec(MLA_KV_RANK, p) for p in range(n_pg)]
            + [page_spec(MLA_ROPE, p) for p in range(n_pg)]
            + [pl.BlockSpec(w_uv.shape, lambda b, g, pt: (0, 0, 0))],
            out_specs=seq_spec(T, MLA_WIDTH),
            scratch_shapes=[pltpu.VMEM((rows, MLA_KV_RANK), F32), pltpu.VMEM((rows, MLA_ROPE), F32),
                            pltpu.VMEM((rows, 1), F32), pltpu.VMEM((rows, 1), F32),
                            pltpu.VMEM((rows, MLA_KV_RANK), F32)]),
        compiler_params=_cparams("parallel", "arbitrary"),
    )(page_table, q_lat, q_pe, ckv_new, kpe_new, *([cache_ckv] * n_pg), *([cache_kpe] * n_pg), w_uv)


S5_SEQS = 8


def _s5_body(u_ref, re0_ref, im0_ref, are_ref, aim_ref, bre_ref, bim_ref, cre_ref, cim_ref, d_ref, wglu_ref,
             bglu_ref, o_ref, re_out_ref, im_out_ref, xre_s, xim_s, re_s, im_s, *, tc):
    tj = pl.program_id(1)
    rows = tc * S5_SEQS

    @pl.when(tj == 0)
    def _():
        re_s[...] = re0_ref[...]
        im_s[...] = im0_ref[...]

    u = u_ref[...].reshape(rows, S5_WIDTH)
    ub = u.astype(BF16)
    xre_s[...] = _dot(ub, bre_ref[...]).reshape(tc, S5_SEQS, S5_FLAT)
    xim_s[...] = _dot(ub, bim_ref[...]).reshape(tc, S5_SEQS, S5_FLAT)
    a_re, a_im = are_ref[...], aim_ref[...]

    def step(t, carry):
        re, im = carry
        n_re = a_re * re - a_im * im + xre_s[t]
        n_im = a_re * im + a_im * re + xim_s[t]
        xre_s[t] = n_re
        xim_s[t] = n_im
        return n_re, n_im

    re, im = lax.fori_loop(0, tc, step, (re_s[...], im_s[...]))
    re_s[...] = re
    im_s[...] = im
    x_re = xre_s[...].reshape(rows, S5_FLAT).astype(BF16)
    x_im = xim_s[...].reshape(rows, S5_FLAT).astype(BF16)
    y = _dot(x_re, cre_ref[...]) - _dot(x_im, cim_ref[...]) + d_ref[...] * u
    y = 0.5 * y * (1.0 + lax.erf(y * (1.0 / math.sqrt(2.0))))
    out = y * jax.nn.sigmoid(_dot(y.astype(BF16), wglu_ref[...]) + bglu_ref[...])
    o_ref[...] = out.reshape(tc, S5_SEQS, S5_WIDTH)

    @pl.when(tj == pl.num_programs(1) - 1)
    def _():
        re_out_ref[...] = re
        im_out_ref[...] = im


def _s5(u_t, re0, im0, prm, tc):
    T, S, _ = u_t.shape
    assert S % S5_SEQS == 0 and T % tc == 0
    names = ("a_re", "a_im", "b_re", "b_im", "c_re", "c_im", "d", "w_glu", "b_glu")
    state_spec = pl.BlockSpec((S5_SEQS, S5_FLAT), lambda i, j: (i, 0))
    row_spec = pl.BlockSpec((tc, S5_SEQS, S5_WIDTH), lambda i, j: (j, i, 0))
    return pl.pallas_call(
        functools.partial(_s5_body, tc=tc),
        out_shape=[jax.ShapeDtypeStruct((T, S, S5_WIDTH), F32), jax.ShapeDtypeStruct((S, S5_FLAT), F32),
                   jax.ShapeDtypeStruct((S, S5_FLAT), F32)],
        grid=(S // S5_SEQS, T // tc),
        in_specs=[row_spec, state_spec, state_spec]
        + [pl.BlockSpec(prm[n].shape, lambda i, j: (0, 0)) for n in names],
        out_specs=[row_spec, state_spec, state_spec],
        scratch_shapes=[pltpu.VMEM((tc, S5_SEQS, S5_FLAT), F32)] * 2 + [pltpu.VMEM((S5_SEQS, S5_FLAT), F32)] * 2,
        compiler_params=_cparams("parallel", "arbitrary"),
    )(u_t, re0, im0, *(prm[n] for n in names))


def _router_body(x_ref, g_ref, sc_ref, sh_ref, wr_ref, br_ref, h_ref, comb_ref, *, sb, tb):
    rows = sb * tb
    x = x_ref[...]
    h = x * lax.rsqrt(jnp.mean(x * x, axis=-1, keepdims=True) + NORM_EPS) * g_ref[...]
    h = h * (1.0 + sc_ref[...]) + sh_ref[...]
    h_ref[...] = h
    logits = jnp.dot(h.reshape(rows, D_MODEL), wr_ref[...], precision=HIGHEST,
                     preferred_element_type=F32) + br_ref[...]
    lane = lax.broadcasted_iota(jnp.int32, (rows, N_EXPERTS), 1)
    work = logits
    vals, sels = [], []
    for _ in range(TOP_K):
        m = jnp.max(work, axis=-1, keepdims=True)
        idx = jnp.min(jnp.where(work == m, lane, N_EXPERTS), axis=-1, keepdims=True)
        sel = lane == idx
        vals.append(m)
        sels.append(sel)
        work = jnp.where(sel, -jnp.inf, work)
    exps = [jnp.exp(v - vals[0]) for v in vals]
    denom = exps[0] + exps[1] + exps[2] + exps[3]
    comb = jnp.zeros((rows, N_EXPERTS), F32)
    for e, sel in zip(exps, sels):
        comb = jnp.where(sel, e / denom, comb)
    comb_ref[...] = comb.reshape(sb, tb, N_EXPERTS)


def _router(x, g, scale, shift, w_r, b_r, rows=512):
    S, T, D = x.shape
    sb, tb = _blocking(S, T, rows)
    row = pl.BlockSpec((sb, tb, D), lambda i, j: (i, j, 0))
    seq = pl.BlockSpec((sb, 1, D), lambda i, j: (i, 0, 0))
    return pl.pallas_call(
        functools.partial(_router_body, sb=sb, tb=tb),
        out_shape=[jax.ShapeDtypeStruct((S, T, D), F32), jax.ShapeDtypeStruct((S, T, N_EXPERTS), F32)],
        grid=(S // sb, T // tb),
        in_specs=[row, pl.BlockSpec((1, D), lambda i, j: (0, 0)), seq, seq,
                  pl.BlockSpec((D, N_EXPERTS), lambda i, j: (0, 0)),
                  pl.BlockSpec((1, N_EXPERTS), lambda i, j: (0, 0))],
        out_specs=[row, pl.BlockSpec((sb, tb, N_EXPERTS), lambda i, j: (i, j, 0))],
        compiler_params=_cparams("parallel", "parallel"),
    )(x, g.reshape(1, D), scale, shift, w_r, b_r.reshape(1, N_EXPERTS))


def _experts_body(h_ref, comb_ref, x_ref, gate_ref, wg_ref, bg_ref, wl_ref, bl_ref, wd_ref, bd_ref, o_ref,
                  hb_s, acc_s, *, sb, tb):
    e = pl.program_id(2)
    rows = sb * tb

    @pl.when(e == 0)
    def _():
        hb_s[...] = h_ref[...].reshape(rows, D_MODEL).astype(BF16)
        acc_s[...] = jnp.zeros(acc_s.shape, F32)

    hb = hb_s[...]
    glu = jnp.minimum(_dot(hb, wg_ref[0]) + bg_ref[0], SWIGLU_LIMIT)
    lin = jnp.clip(_dot(hb, wl_ref[0]) + bl_ref[0], -SWIGLU_LIMIT, SWIGLU_LIMIT)
    act = glu * jax.nn.sigmoid(SWIGLU_ALPHA * glu) * (lin + 1.0)
    y = _dot(act.astype(BF16), wd_ref[0]) + bd_ref[0]
    comb = comb_ref[...].reshape(rows, N_EXPERTS)
    lane = lax.broadcasted_iota(jnp.int32, (rows, N_EXPERTS), 1)
    w_e = jnp.sum(jnp.where(lane == e, comb, 0.0), axis=-1, keepdims=True)
    acc_s[...] = acc_s[...] + w_e * y

    @pl.when(e == pl.num_programs(2) - 1)
    def _():
        o_ref[...] = x_ref[...] + gate_ref[...] * acc_s[...].reshape(sb, tb, D_MODEL)


def _experts(h, comb, x, gate, prm, rows=1024):
    S, T, D = x.shape
    sb, tb = _blocking(S, T, rows)
    F = D_FF_EXPERT
    row = pl.BlockSpec((sb, tb, D), lambda i, j, e: (i, j, 0))
    return pl.pallas_call(
        functools.partial(_experts_body, sb=sb, tb=tb),
        out_shape=jax.ShapeDtypeStruct((S, T, D), F32),
        grid=(S // sb, T // tb, N_EXPERTS),
        in_specs=[row, pl.BlockSpec((sb, tb, N_EXPERTS), lambda i, j, e: (i, j, 0)), row,
                  pl.BlockSpec((sb, 1, D), lambda i, j, e: (i, 0, 0)),
                  pl.BlockSpec((1, D, F), lambda i, j, e: (e, 0, 0)),
                  pl.BlockSpec((1, 1, F), lambda i, j, e: (e, 0, 0)),
                  pl.BlockSpec((1, D, F), lambda i, j, e: (e, 0, 0)),
                  pl.BlockSpec((1, 1, F), lambda i, j, e: (e, 0, 0)),
                  pl.BlockSpec((1, F, D), lambda i, j, e: (e, 0, 0)),
                  pl.BlockSpec((1, 1, D), lambda i, j, e: (e, 0, 0))],
        out_specs=row,
        scratch_shapes=[pltpu.VMEM((sb * tb, D), BF16), pltpu.VMEM((sb * tb, D), F32)],
        compiler_params=_cparams("parallel", "parallel", "arbitrary"),
    )(h, comb, x, gate, prm["w_g"], prm["b_g"], prm["w_l"], prm["b_l"], prm["w_d"], prm["b_d"])


def _rw_column_order():
    W, dl, al = RW_WIDTH, RW_DECAY_LORA, RW_AAA_LORA
    r = np.arange(0, W)
    wd = np.arange(W, W + dl)
    k = np.arange(W + dl, 2 * W + dl)
    v = np.arange(2 * W + dl, 3 * W + dl)
    rest = np.arange(3 * W + dl, RW_PROJ)
    return np.concatenate([r, k, v, wd, rest])


def _q_up_column_order():
    per = MLA_NOPE + MLA_ROPE
    nope = np.concatenate([np.arange(h * per, h * per + MLA_NOPE) for h in range(MLA_HEADS)])
    rope = np.concatenate([np.arange(h * per + MLA_NOPE, (h + 1) * per) for h in range(MLA_HEADS)])
    return np.concatenate([nope, rope])


def _block_diag(blocks):
    G, a, b = blocks.shape
    eye = jnp.eye(G, dtype=blocks.dtype)
    return (eye[:, None, :, None] * blocks[:, :, None, :]).reshape(G * a, G * b)


def _even_params(P, j):
    order = _rw_column_order()
    w_in = P["ev_w_in"][j]
    row = lambda a: a.reshape(1, -1).astype(F32)
    head = np.arange(RW_WIDTH) // RW_HEAD_DIM
    return dict(
        w_in=jnp.concatenate([w_in[:, :RW_PROJ][:, order], w_in[:, RW_PROJ:]], axis=1).astype(BF16),
        w_out_rw=P["ev_w_out"][j][:RW_WIDTH].astype(BF16),
        w_out_sb=P["ev_w_out"][j][RW_WIDTH:].astype(BF16),
        rw=dict(mu=row(P["rw_mu"][j][order]), w0=row(P["rw_w0"][j]), w2=P["rw_w2"][j].astype(BF16),
                a0=row(P["rw_a0"][j]), a2=P["rw_a2"][j].astype(BF16), g2=P["rw_g2"][j].astype(BF16),
                k_k=row(P["rw_k_k"][j]), k_a=row(P["rw_k_a"][j]), r_k=row(P["rw_r_k"][j]),
                ln_w=row(P["rw_ln_w"][j]), ln_b=row(P["rw_ln_b"][j]),
                ones=jnp.asarray((head[:, None] == head[None, :]).astype(np.float32))))


def _odd_params(P, j):
    row = lambda a: a.reshape(1, -1).astype(F32)
    w_in = P["od_w_in"][j]
    pad = jnp.zeros((D_MODEL, KR_PAD - MLA_ROPE), w_in.dtype)
    lr, li = P["s5_lam_re"][j].astype(F32), P["s5_lam_im"][j].astype(F32)
    dt = jnp.exp(P["s5_log_step"][j].astype(F32))[:, None]
    mag = jnp.exp(lr * dt)
    ab_re, ab_im = mag * jnp.cos(li * dt), mag * jnp.sin(li * dt)
    den = lr * lr + li * li
    f_re = ((ab_re - 1.0) * lr + ab_im * li) / den
    f_im = (ab_im * lr - (ab_re - 1.0) * li) / den
    b_re, b_im = P["s5_b_re"][j].astype(F32), P["s5_b_im"][j].astype(F32)
    bb_re = f_re[..., None] * b_re - f_im[..., None] * b_im
    bb_im = f_re[..., None] * b_im + f_im[..., None] * b_re
    to_in = lambda a: _block_diag(jnp.swapaxes(a, 1, 2)).astype(BF16)
    to_out = lambda a: _block_diag(jnp.swapaxes(a, 1, 2)).astype(BF16)
    return dict(
        w_in=jnp.concatenate([w_in, pad], axis=1).astype(BF16),
        w_out_s5=P["od_w_out"][j][:S5_WIDTH].astype(BF16),
        w_out_mla=P["od_w_out"][j][S5_WIDTH:].astype(BF16),
        s5=dict(a_re=row(ab_re), a_im=row(ab_im), b_re=to_in(bb_re), b_im=to_in(bb_im),
                c_re=to_out(P["s5_c_re"][j].astype(F32)), c_im=to_out(P["s5_c_im"][j].astype(F32)),
                d=row(P["s5_d"][j]), w_glu=P["s5_w_glu"][j].astype(BF16), b_glu=row(P["s5_b_glu"][j])),
        mla=dict(q_g=row(P["mla_q_norm_g"][j]), w_q=P["mla_w_q_up"][j][:, _q_up_column_order()].astype(BF16),
                 kv_g=row(P["mla_kv_norm_g"][j]),
                 w_uk=jnp.transpose(P["mla_w_uk"][j], (1, 2, 0)).astype(BF16),
                 w_uv=jnp.transpose(P["mla_w_uv"][j], (1, 0, 2)).astype(BF16)))


def _moe_params(P, l):
    w_gu, b_gu = P["moe_w_gu"][l], P["moe_b_gu"][l]
    return dict(w_r=P["moe_w_router"][l].astype(F32), b_r=P["moe_b_router"][l].astype(F32),
                w_g=w_gu[:, :, 0::2].astype(BF16), w_l=w_gu[:, :, 1::2].astype(BF16),
                b_g=b_gu[:, None, 0::2].astype(F32), b_l=b_gu[:, None, 1::2].astype(F32),
                w_d=P["moe_w_down"][l].astype(BF16), b_d=P["moe_b_down"][l][:, None, :].astype(F32))


def _rope_tables(q_pos):
    inv = jnp.power(ROPE_THETA, -jnp.arange(ROPE_HALF, dtype=F32) / ROPE_HALF)
    ang = q_pos.astype(F32)[:, None] * inv[None, :]
    cos, sin = jnp.cos(ang), jnp.sin(ang)
    q_rep = MLA_HEADS * MLA_ROPE // ROPE_HALF
    k_rep = KR_PAD // ROPE_HALF
    return jnp.tile(cos, (1, q_rep)), jnp.tile(sin, (1, q_rep)), jnp.tile(cos, (1, k_rep)), jnp.tile(sin, (1, k_rep))


def _pad_rows(a, n):
    return jnp.pad(a, ((0, 0), (0, n - a.shape[1]), (0, 0)))


def _trunk(x, c, q_pos, shift0, wkv0, s5re0, s5im0, paged, P, prepared):
    S, T, D = x.shape
    new = {name: [] for name in ("sb_k", "sb_v", "ckv", "kpe", "wkv", "shift", "s5re", "s5im")}
    tables = _rope_tables(q_pos)
    rw_sb, rw_tb = (S5_SEQS, T) if T < 256 else (2, 256)
    for l in range(DEPTH):
        j = l // 2
        mod = _linear(c[None], P["w_mod"][l].astype(BF16), b=P["b_mod"][l], silu=True, tn=1536)[0]
        sh1, sc1, g1, sh2, sc2, g2 = (mod[0, :, i * D:(i + 1) * D][:, None, :] for i in range(6))
        if l % 2 == 0:
            ep = prepared["even"][j]
            rw, q, k, v = _linear(x, ep["w_in"], norm_g=P["norm_mix_g"][l], scale=sc1, shift=sh1,
                                  splits=[RW_PROJ, SB_WIDTH, SB_WIDTH, SB_WIDTH], rows=256)
            shift = _normmod(x[:, T - 1:T], P["norm_mix_g"][l], sc1, sh1)
            if shift0 is None:
                prev = jnp.zeros((S, 1, RW_PROJ), F32)
            else:
                prev = _linear(shift0[:, j][None], ep["w_in"][:, :RW_PROJ])[0].reshape(S, 1, RW_PROJ)
            rw_out, wkv = _rwkv(rw, prev, None if wkv0 is None else wkv0[:, j], ep["rw"], sb=rw_sb, tb=rw_tb)
            if paged is None:
                sb_out = _sb_prompt(q, k, v)
            else:
                sb_out = _sb_paged(q, _pad_rows(k, PAGE_SIZE), _pad_rows(v, PAGE_SIZE), paged["sb_k"],
                                   paged["sb_v"], paged["page_table"], j)
            x = _linear(rw_out, ep["w_out_rw"], x2=sb_out, w2=ep["w_out_sb"], res=x, gate=g1)[0]
            new["shift"].append(shift[:, 0])
            new["wkv"].append(wkv)
            new["sb_k"].append(k.reshape(S, T, SB_HEADS, SB_HEAD_DIM))
            new["sb_v"].append(v.reshape(S, T, SB_HEADS, SB_HEAD_DIM))
        else:
            op = prepared["odd"][j]
            u, qd, kvd, kr = _linear(x, op["w_in"], norm_g=P["norm_mix_g"][l], scale=sc1, shift=sh1,
                                     splits=[S5_WIDTH, MLA_Q_RANK, MLA_KV_RANK, KR_PAD], rows=256)
            re0 = jnp.zeros((S, S5_FLAT), F32) if s5re0 is None else s5re0[:, j].reshape(S, S5_FLAT)
            im0 = jnp.zeros((S, S5_FLAT), F32) if s5im0 is None else s5im0[:, j].reshape(S, S5_FLAT)
            s5_t, s_re, s_im = _s5(jnp.swapaxes(u, 0, 1), re0, im0, op["s5"], tc=min(T, 128))
            s5_out = jnp.swapaxes(s5_t, 0, 1)
            q_lat, q_pe, ckv, kpe = _mla_prep(qd, kvd, kr, tables, op["mla"])
            if paged is None:
                mla_out = _mla_prompt(q_lat, q_pe, ckv, kpe, op["mla"]["w_uv"])
            else:
                mla_out = _mla_paged(q_lat, q_pe, _pad_rows(ckv, PAGE_SIZE), _pad_rows(kpe, PAGE_SIZE),
                                     paged["ckv"], paged["kpe"], paged["page_table"], j, op["mla"]["w_uv"])
            x = _linear(s5_out, op["w_out_s5"], x2=mla_out, w2=op["w_out_mla"], res=x, gate=g1)[0]
            new["s5re"].append(s_re.reshape(S, S5_GROUPS, S5_STATE))
            new["s5im"].append(s_im.reshape(S, S5_GROUPS, S5_STATE))
            new["ckv"].append(ckv)
            new["kpe"].append(kpe[..., :MLA_ROPE])
        mp = prepared["moe"][l]
        h, comb = _router(x, P["norm_ffn_g"][l], sc2, sh2, mp["w_r"], mp["b_r"])
        x = _experts(h, comb, x, g2, mp)
    y = _normmod(x, P["norm_final_g"])
    return y, {name: jnp.stack(vals, axis=1) for name, vals in new.items()}


def kernel(x_prompt, x_sample, cache_sb_k, cache_sb_v, cache_mla_ckv, cache_mla_kpe, state_rwkv_wkv,
           state_rwkv_shift, state_s5_re, state_s5_im, page_table, c_prompt, c_sample, w_mod, b_mod,
           norm_mix_g, norm_ffn_g, norm_final_g, ev_w_in, ev_w_out, rw_mu, rw_w0, rw_w2, rw_a0, rw_a2, rw_g2,
           rw_k_k, rw_k_a, rw_r_k, rw_ln_w, rw_ln_b, od_w_in, od_w_out, s5_lam_re, s5_lam_im, s5_b_re, s5_b_im,
           s5_c_re, s5_c_im, s5_d, s5_log_step, s5_w_glu, s5_b_glu, mla_q_norm_g, mla_w_q_up, mla_kv_norm_g,
           mla_w_uk, mla_w_uv, moe_w_router, moe_b_router, moe_w_gu, moe_b_gu, moe_w_down, moe_b_down):
    P = dict(w_mod=w_mod, b_mod=b_mod, norm_mix_g=norm_mix_g, norm_ffn_g=norm_ffn_g, norm_final_g=norm_final_g,
             ev_w_in=ev_w_in, ev_w_out=ev_w_out, rw_mu=rw_mu, rw_w0=rw_w0, rw_w2=rw_w2, rw_a0=rw_a0, rw_a2=rw_a2,
             rw_g2=rw_g2, rw_k_k=rw_k_k, rw_k_a=rw_k_a, rw_r_k=rw_r_k, rw_ln_w=rw_ln_w, rw_ln_b=rw_ln_b,
             od_w_in=od_w_in, od_w_out=od_w_out, s5_lam_re=s5_lam_re, s5_lam_im=s5_lam_im, s5_b_re=s5_b_re,
             s5_b_im=s5_b_im, s5_c_re=s5_c_re, s5_c_im=s5_c_im, s5_d=s5_d, s5_log_step=s5_log_step,
             s5_w_glu=s5_w_glu, s5_b_glu=s5_b_glu, mla_q_norm_g=mla_q_norm_g, mla_w_q_up=mla_w_q_up,
             mla_kv_norm_g=mla_kv_norm_g, mla_w_uk=mla_w_uk, mla_w_uv=mla_w_uv, moe_w_router=moe_w_router,
             moe_b_router=moe_b_router, moe_w_gu=moe_w_gu, moe_b_gu=moe_b_gu, moe_w_down=moe_w_down,
             moe_b_down=moe_b_down)
    n_even, n_odd = (DEPTH + 1) // 2, DEPTH // 2
    prepared = dict(even=[_even_params(P, j) for j in range(n_even)],
                    odd=[_odd_params(P, j) for j in range(n_odd)],
                    moe=[_moe_params(P, l) for l in range(DEPTH)])
    sp = x_prompt.shape[1]
    y_prompt, pn = _trunk(x_prompt, c_prompt, jnp.arange(sp, dtype=jnp.int32), None, None, None, None, None,
                          P, prepared)
    past_len = page_table.shape[1] * cache_sb_k.shape[2]
    paged = dict(page_table=page_table, sb_k=cache_sb_k, sb_v=cache_sb_v, ckv=cache_mla_ckv, kpe=cache_mla_kpe)
    y_sample, sn = _trunk(x_sample, c_sample, past_len + jnp.arange(x_sample.shape[1], dtype=jnp.int32),
                          state_rwkv_shift, state_rwkv_wkv, state_s5_re, state_s5_im, paged, P, prepared)
    return (y_prompt, y_sample,
            pn["sb_k"], pn["sb_v"], pn["ckv"], pn["kpe"], pn["wkv"], pn["shift"], pn["s5re"], pn["s5im"],
            sn["sb_k"], sn["sb_v"], sn["ckv"], sn["kpe"], sn["wkv"], sn["shift"], sn["s5re"], sn["s5im"])
```

```python
import functools
import math

import numpy as np
import jax
import jax.numpy as jnp
from jax import lax
from jax.experimental import pallas as pl
from jax.experimental.pallas import tpu as pltpu

F32 = jnp.float32
BF16 = jnp.bfloat16
HIGHEST = lax.Precision.HIGHEST

D_MODEL = 1024
DEPTH = 4
PAGE_SIZE = 128
NORM_EPS = 1e-6

RW_HEADS = 8
RW_HEAD_DIM = 64
RW_WIDTH = RW_HEADS * RW_HEAD_DIM
RW_DECAY_LORA = 64
RW_AAA_LORA = 64
RW_GATE_LORA = 128
RW_PROJ = 3 * RW_WIDTH + RW_DECAY_LORA + RW_AAA_LORA + RW_GATE_LORA
RW_GN_EPS = 64e-5

SB_HEADS = 8
SB_HEAD_DIM = 64
SB_WIDTH = SB_HEADS * SB_HEAD_DIM
SB_SCALE = SB_HEAD_DIM ** -0.5

S5_GROUPS = 32
S5_GROUP_CH = 16
S5_WIDTH = S5_GROUPS * S5_GROUP_CH
S5_STATE = 64
S5_FLAT = S5_GROUPS * S5_STATE

MLA_HEADS = 8
MLA_Q_RANK = 384
MLA_KV_RANK = 256
MLA_NOPE = 64
MLA_ROPE = 32
MLA_V = 64
MLA_WIDTH = MLA_HEADS * MLA_V
MLA_SCALE = (MLA_NOPE + MLA_ROPE) ** -0.5
ROPE_THETA = 10000.0
ROPE_HALF = MLA_ROPE // 2
KR_PAD = 128

N_EXPERTS = 32
TOP_K = 4
D_FF_EXPERT = 512
SWIGLU_LIMIT = 7.0
SWIGLU_ALPHA = 1.702

VMEM_LIMIT_BYTES = 56 * 1024 * 1024
QBLK = 128
MLA_KBLK = 256
PAGES_PER_STEP = 8


def _cparams(*sem):
    return pltpu.CompilerParams(dimension_semantics=sem, vmem_limit_bytes=VMEM_LIMIT_BYTES)


def _blocking(S, T, rows):
    if T >= rows:
        assert T % rows == 0
        return 1, rows
    sb = max(1, min(S, rows // T))
    assert S % sb == 0
    return sb, T


def _softplus(x):
    return jnp.maximum(x, 0.0) + jnp.log1p(jnp.exp(-jnp.abs(x)))


def _dot(a, b):
    return jnp.dot(a, b, preferred_element_type=F32)


def _dot_nt(a, b):
    return lax.dot_general(a, b, (((1,), (1,)), ((), ())), preferred_element_type=F32)


def _split_hi_lo(x):
    hi = x.astype(BF16)
    lo = (x - hi.astype(F32)).astype(BF16)
    return hi, lo


def _linear_body(*refs, n_out, splits, has_norm, has_mod, silu, has_x2, has_b, has_res, sb, tb):
    outs = refs[len(refs) - n_out:]
    it = iter(refs[:len(refs) - n_out])
    x = next(it)[...]
    if has_norm:
        g_ref = next(it)
        x = x * lax.rsqrt(jnp.mean(x * x, axis=-1, keepdims=True) + NORM_EPS) * g_ref[...]
    if has_mod:
        sc_ref, sh_ref = next(it), next(it)
        x = x * (1.0 + sc_ref[...]) + sh_ref[...]
    if silu:
        x = x * jax.nn.sigmoid(x)
    w_ref = next(it)
    rows = sb * tb
    y = _dot(x.reshape(rows, x.shape[-1]).astype(BF16), w_ref[...])
    if has_x2:
        x2_ref, w2_ref = next(it), next(it)
        x2 = x2_ref[...]
        y = y + _dot(x2.reshape(rows, x2.shape[-1]).astype(BF16), w2_ref[...])
    if has_b:
        y = y + next(it)[...]
    if has_res:
        res_ref, gate_ref = next(it), next(it)
    off = 0
    for o_ref, n in zip(outs, splits):
        piece = y[:, off:off + n].reshape(sb, tb, n)
        if has_res:
            piece = res_ref[...] + gate_ref[...] * piece
        o_ref[...] = piece
        off += n


def _linear(x, w, *, b=None, norm_g=None, scale=None, shift=None, silu=False, x2=None, w2=None,
            res=None, gate=None, splits=None, rows=512, tn=None):
    S, T, K = x.shape
    N = w.shape[1]
    splits = list(splits) if splits is not None else [N]
    tn = N if tn is None else tn
    assert N % tn == 0 and (tn == N or len(splits) == 1)
    assert res is None or len(splits) == 1
    sb, tb = _blocking(S, T, rows)
    grid = (S // sb, T // tb, N // tn)

    def row_spec(width):
        return pl.BlockSpec((sb, tb, width), lambda i, j, n: (i, j, 0))

    def seq_spec(width):
        return pl.BlockSpec((sb, 1, width), lambda i, j, n: (i, 0, 0))

    args, specs = [x], [row_spec(K)]
    if norm_g is not None:
        args.append(norm_g.reshape(1, K))
        specs.append(pl.BlockSpec((1, K), lambda i, j, n: (0, 0)))
    if scale is not None:
        args += [scale, shift]
        specs += [seq_spec(K), seq_spec(K)]
    args.append(w)
    specs.append(pl.BlockSpec((K, tn), lambda i, j, n: (0, n)))
    if x2 is not None:
        K2 = x2.shape[-1]
        args += [x2, w2]
        specs += [row_spec(K2), pl.BlockSpec((K2, tn), lambda i, j, n: (0, n))]
    if b is not None:
        args.append(b.reshape(1, N))
        specs.append(pl.BlockSpec((1, tn), lambda i, j, n: (0, n)))
    if res is not None:
        args += [res, gate]
        specs += [pl.BlockSpec((sb, tb, tn), lambda i, j, n: (i, j, n)),
                  pl.BlockSpec((sb, 1, tn), lambda i, j, n: (i, 0, n))]
    if tn == N:
        out_specs = [row_spec(n) for n in splits]
        blk_splits = splits
    else:
        out_specs = [pl.BlockSpec((sb, tb, tn), lambda i, j, n: (i, j, n))]
        blk_splits = [tn]
    body = functools.partial(
        _linear_body, n_out=len(splits), splits=blk_splits, has_norm=norm_g is not None,
        has_mod=scale is not None, silu=silu, has_x2=x2 is not None, has_b=b is not None,
        has_res=res is not None, sb=sb, tb=tb)
    return pl.pallas_call(
        body,
        out_shape=[jax.ShapeDtypeStruct((S, T, n), F32) for n in splits],
        grid=grid, in_specs=specs, out_specs=out_specs,
        compiler_params=_cparams("parallel", "parallel", "arbitrary"),
        name="linear",
    )(*args)


def _normmod_body(*refs, has_mod):
    it = iter(refs)
    x = next(it)[...]
    g_ref = next(it)
    y = x * lax.rsqrt(jnp.mean(x * x, axis=-1, keepdims=True) + NORM_EPS) * g_ref[...]
    if has_mod:
        sc_ref, sh_ref = next(it), next(it)
        y = y * (1.0 + sc_ref[...]) + sh_ref[...]
    next(it)[...] = y


def _normmod(x, g, scale=None, shift=None, rows=512):
    S, T, K = x.shape
    sb, tb = _blocking(S, T, rows)
    args = [x, g.reshape(1, K)]
    specs = [pl.BlockSpec((sb, tb, K), lambda i, j: (i, j, 0)), pl.BlockSpec((1, K), lambda i, j: (0, 0))]
    if scale is not None:
        args += [scale, shift]
        specs += [pl.BlockSpec((sb, 1, K), lambda i, j: (i, 0, 0))] * 2
    return pl.pallas_call(
        functools.partial(_normmod_body, has_mod=scale is not None),
        out_shape=jax.ShapeDtypeStruct((S, T, K), F32),
        grid=(S // sb, T // tb), in_specs=specs,
        out_specs=pl.BlockSpec((sb, tb, K), lambda i, j: (i, j, 0)),
        compiler_params=_cparams("parallel", "parallel"),
        name="normmod",
    )(*args)


RW_COLS = RW_PROJ


def _rwkv_body(*refs, has_state, sb, tb, chunk):
    it = iter(refs)
    rw_ref, prev_ref = next(it), next(it)
    wkv0_ref = next(it) if has_state else None
    (mu_ref, w0_ref, w2_ref, a0_ref, a2_ref, g2_ref, kk_w_ref, ka_ref, rk_ref, lnw_ref, lnb_ref,
     ones_ref) = (next(it) for _ in range(12))
    out_ref, wkv_ref = next(it), next(it)
    carry_s, state_s, r_s, w_s, k_s, v_s, kk_s, b_s, y_s = (next(it) for _ in range(9))
    H, N, W = RW_HEADS, RW_HEAD_DIM, RW_WIDTH
    HALF = W // 2
    rows = sb * tb
    tj = pl.program_id(1)

    @pl.when(tj == 0)
    def _():
        carry_s[...] = prev_ref[...]
        if has_state:
            for h in range(H):
                state_s[:, :, h * N:(h + 1) * N] = wkv0_ref[:, h]
        else:
            state_s[...] = jnp.zeros(state_s.shape, F32)

    x = rw_ref[...]
    rolled = pltpu.roll(x.reshape(rows, RW_COLS), 1, 0).reshape(sb, tb, RW_COLS)
    tpos = lax.broadcasted_iota(jnp.int32, (sb, tb, 1), 1)
    shifted = jnp.where(tpos == 0, carry_s[...], rolled)
    carry_s[...] = x[:, tb - 1:tb, :]
    xm = (x + (shifted - x) * mu_ref[...]).reshape(rows, RW_COLS)
    r, k, v = xm[:, 0:W], xm[:, W:2 * W], xm[:, 2 * W:3 * W]
    o = 3 * W
    wd = xm[:, o:o + RW_DECAY_LORA]
    ad = xm[:, o + RW_DECAY_LORA:o + RW_DECAY_LORA + RW_AAA_LORA]
    gd = xm[:, o + RW_DECAY_LORA + RW_AAA_LORA:RW_COLS]

    w_log = -_softplus(-(w0_ref[...] + _dot(jnp.tanh(wd).astype(BF16), w2_ref[...]))) - 0.5
    decay = jnp.exp(-jnp.exp(w_log))
    a = jax.nn.sigmoid(a0_ref[...] + _dot(ad.astype(BF16), a2_ref[...]))
    g = _dot(jax.nn.sigmoid(gd).astype(BF16), g2_ref[...])
    ones_blk = ones_ref[...]
    kk = k * kk_w_ref[...]
    kk_sq = jnp.dot(kk * kk, ones_blk, precision=HIGHEST, preferred_element_type=F32)
    kk = kk / jnp.maximum(jnp.sqrt(kk_sq), 1e-12)
    k = k * (1.0 + (a - 1.0) * ka_ref[...])
    for dst, src in ((r_s, r), (w_s, decay), (k_s, k), (v_s, v), (kk_s, kk), (b_s, kk * a)):
        dst[...] = src.reshape(sb, tb, W)

    lane = lax.broadcasted_iota(jnp.int32, (N, W), 1)
    sub = lax.broadcasted_iota(jnp.int32, (N, W), 0)
    diag = (lane % N == sub).astype(F32)
    pr = lax.broadcasted_iota(jnp.int32, (HALF, HALF), 0) // N
    pc = lax.broadcasted_iota(jnp.int32, (HALF, HALF), 1) // N
    head_ones = (pr == pc).astype(BF16)

    def head_sums(xs):
        m = sb * N
        parts = []
        for xv in xs:
            parts += list(_split_hi_lo(xv.reshape(m, W)))
        stacked = jnp.concatenate(parts, axis=0)
        res = jnp.concatenate([_dot(stacked[:, c:c + HALF], head_ones) for c in (0, HALF)], axis=-1)
        return [(res[2 * i * m:(2 * i + 1) * m] + res[(2 * i + 1) * m:(2 * i + 2) * m]).reshape(sb, N, W)
                for i in range(len(xs))]

    def chunk_body(c, carry):
        t0 = pl.multiple_of(c * chunk, chunk)
        y_rows = []
        st = state_s[...]
        for i in range(chunk):
            ts = pl.ds(t0 + i, 1)
            s_kk, v_col = head_sums([st * kk_s[:, ts, :], diag * v_s[:, ts, :]])
            st = st * w_s[:, ts, :] - s_kk * b_s[:, ts, :] + v_col * k_s[:, ts, :]
            y_b, = head_sums([st * r_s[:, ts, :]])
            y_rows.append(jnp.sum(y_b * diag, axis=1, keepdims=True))
        state_s[...] = st
        y_s[:, pl.ds(t0, chunk), :] = jnp.concatenate(y_rows, axis=1)
        return carry

    lax.fori_loop(0, tb // chunk, chunk_body, 0)

    y = y_s[...].reshape(rows, W)
    avg_blk = ones_blk * (1.0 / N)
    mean = jnp.dot(y, avg_blk, precision=HIGHEST, preferred_element_type=F32)
    dev = y - mean
    var = jnp.dot(dev * dev, avg_blk, precision=HIGHEST, preferred_element_type=F32)
    yn = dev * lax.rsqrt(var + RW_GN_EPS) * lnw_ref[...] + lnb_ref[...]
    bonus = jnp.dot(r * k * rk_ref[...], ones_blk, precision=HIGHEST, preferred_element_type=F32) * v
    out_ref[...] = ((yn + bonus) * g).reshape(sb, tb, W)

    @pl.when(tj == pl.num_programs(1) - 1)
    def _():
        for h in range(H):
            wkv_ref[:, h] = state_s[:, :, h * N:(h + 1) * N]


def _rwkv(rw, prev, wkv0, prm, *, sb, tb):
    S, T, _ = rw.shape
    H, N, W = RW_HEADS, RW_HEAD_DIM, RW_WIDTH
    chunk = min(tb, 16)
    has_state = wkv0 is not None
    args = [rw, prev]
    specs = [pl.BlockSpec((sb, tb, RW_COLS), lambda i, j: (i, j, 0)),
             pl.BlockSpec((sb, 1, RW_COLS), lambda i, j: (i, 0, 0))]
    if has_state:
        args.append(wkv0)
        specs.append(pl.BlockSpec((sb, H, N, N), lambda i, j: (i, 0, 0, 0)))
    for name in ("mu", "w0", "w2", "a0", "a2", "g2", "k_k", "k_a", "r_k", "ln_w", "ln_b", "ones"):
        arr = prm[name]
        args.append(arr)
        specs.append(pl.BlockSpec(arr.shape, lambda i, j: (0, 0)))
    tok_scratch = pltpu.VMEM((sb, tb, W), F32)
    return pl.pallas_call(
        functools.partial(_rwkv_body, has_state=has_state, sb=sb, tb=tb, chunk=chunk),
        out_shape=[jax.ShapeDtypeStruct((S, T, W), F32), jax.ShapeDtypeStruct((S, H, N, N), F32)],
        grid=(S // sb, T // tb), in_specs=specs,
        out_specs=[pl.BlockSpec((sb, tb, W), lambda i, j: (i, j, 0)),
                   pl.BlockSpec((sb, H, N, N), lambda i, j: (i, 0, 0, 0))],
        scratch_shapes=[pltpu.VMEM((sb, 1, RW_COLS), F32), pltpu.VMEM((sb, N, W), F32)]
        + [tok_scratch] * 7,
        compiler_params=_cparams("parallel", "arbitrary"),
        name="rwkv7",
    )(*args)


def _sb_scores(z, valid):
    log_beta = -_softplus(-z)
    log_rem = log_beta - z
    if valid is not None:
        log_rem = jnp.where(valid, log_rem, 0.0)
    return log_beta, log_rem


def _sb_weights(log_beta, log_rem, valid, tri, run):
    hi, lo = _split_hi_lo(log_rem)
    after = _dot(hi, tri) + _dot(lo, tri)
    a = jnp.exp(log_beta + after + run)
    if valid is not None:
        a = jnp.where(valid, a, 0.0)
    return a.astype(BF16), run + jnp.sum(log_rem, axis=-1, keepdims=True)


def _later_key_matrix(n):
    j = lax.broadcasted_iota(jnp.int32, (n, n), 0)
    s = lax.broadcasted_iota(jnp.int32, (n, n), 1)
    return (j > s).astype(BF16)


def _sb_prompt_body(q_ref, k_ref, v_ref, o_ref, acc_s, run_s):
    qi = pl.program_id(1)
    D = SB_HEAD_DIM
    row = lax.broadcasted_iota(jnp.int32, (QBLK, QBLK), 0)
    col = lax.broadcasted_iota(jnp.int32, (QBLK, QBLK), 1)
    tri = _later_key_matrix(QBLK)
    acc_s[...] = jnp.zeros(acc_s.shape, F32)
    run_s[...] = jnp.zeros(run_s.shape, F32)

    def kb_body(i, carry):
        off = pl.multiple_of((qi - i) * QBLK, QBLK)
        valid = (col - row) < i * QBLK
        for h in range(SB_HEADS):
            hs = slice(h * D, (h + 1) * D)
            qh = (q_ref[0, :, hs] * SB_SCALE).astype(BF16)
            kh = k_ref[0, pl.ds(off, QBLK), hs].astype(BF16)
            vh = v_ref[0, pl.ds(off, QBLK), hs].astype(BF16)
            log_beta, log_rem = _sb_scores(_dot_nt(qh, kh), valid)
            a, run = _sb_weights(log_beta, log_rem, valid, tri, run_s[h])
            acc_s[h] = acc_s[h] + _dot(a, vh)
            run_s[h] = run
        return carry

    lax.fori_loop(0, qi + 1, kb_body, 0)
    for h in range(SB_HEADS):
        o_ref[0, :, h * D:(h + 1) * D] = acc_s[h]


def _sb_prompt(q, k, v):
    B, T, W = q.shape
    return pl.pallas_call(
        _sb_prompt_body,
        out_shape=jax.ShapeDtypeStruct((B, T, W), F32),
        grid=(B, T // QBLK),
        in_specs=[pl.BlockSpec((1, QBLK, W), lambda b, i: (b, i, 0)),
                  pl.BlockSpec((1, T, W), lambda b, i: (b, 0, 0)),
                  pl.BlockSpec((1, T, W), lambda b, i: (b, 0, 0))],
        out_specs=pl.BlockSpec((1, QBLK, W), lambda b, i: (b, i, 0)),
        scratch_shapes=[pltpu.VMEM((SB_HEADS, QBLK, SB_HEAD_DIM), F32), pltpu.VMEM((SB_HEADS, QBLK, 1), F32)],
        compiler_params=_cparams("parallel", "arbitrary"),
        name="sb_prompt",
    )(q, k, v)


def _sb_paged_body(pt_ref, q_ref, kn_ref, vn_ref, *refs, n_pg):
    k_pages, v_pages = refs[:n_pg], refs[n_pg:2 * n_pg]
    o_ref, q_s, acc_s, run_s = refs[2 * n_pg:]
    g = pl.program_id(1)
    D, T, W = SB_HEAD_DIM, q_ref.shape[1], SB_WIDTH
    rows = SB_HEADS * T
    tri = _later_key_matrix(PAGE_SIZE)

    @pl.when(g == 0)
    def _():
        q_s[...] = jnp.zeros(q_s.shape, F32)
        for h in range(SB_HEADS):
            q_s[h * T:(h + 1) * T, h * D:(h + 1) * D] = q_ref[0, :, h * D:(h + 1) * D] * SB_SCALE
        t_pos = lax.broadcasted_iota(jnp.int32, (rows, PAGE_SIZE), 0) % T
        col = lax.broadcasted_iota(jnp.int32, (rows, PAGE_SIZE), 1)
        valid = col < t_pos
        log_beta, log_rem = _sb_scores(_dot_nt(q_s[...].astype(BF16), kn_ref[0].astype(BF16)), valid)
        a, run = _sb_weights(log_beta, log_rem, valid, tri, jnp.zeros((rows, 1), F32))
        acc_s[...] = _dot(a, vn_ref[0].astype(BF16))
        run_s[...] = run

    qbd = q_s[...].astype(BF16)
    scores = [_sb_scores(_dot(qbd, k_pages[p][...].reshape(W, PAGE_SIZE).astype(BF16)), None)
              for p in range(n_pg)]
    run = run_s[...]
    acc = acc_s[...]
    for p in range(n_pg):
        a, run = _sb_weights(scores[p][0], scores[p][1], None, tri, run)
        acc = acc + _dot_nt(a, v_pages[p][...].reshape(W, PAGE_SIZE).astype(BF16))
    acc_s[...] = acc
    run_s[...] = run

    @pl.when(g == pl.num_programs(1) - 1)
    def _():
        for h in range(SB_HEADS):
            o_ref[0, :, h * D:(h + 1) * D] = acc_s[h * T:(h + 1) * T, h * D:(h + 1) * D]


def _sb_paged(q, k_new, v_new, cache_kt, cache_vt, page_table, j):
    B, T, W = q.shape
    n_pages = page_table.shape[1]
    n_pg = PAGES_PER_STEP
    assert n_pages % n_pg == 0
    rows = SB_HEADS * T

    def page_spec(p):
        return pl.BlockSpec((None, None, SB_HEADS, SB_HEAD_DIM, PAGE_SIZE),
                            lambda b, g, pt: (pt[b, n_pages - 1 - (g * n_pg + p)], j, 0, 0, 0))

    new_spec = pl.BlockSpec((1, PAGE_SIZE, W), lambda b, g, pt: (b, 0, 0))
    row_spec = pl.BlockSpec((1, T, W), lambda b, g, pt: (b, 0, 0))
    return pl.pallas_call(
        functools.partial(_sb_paged_body, n_pg=n_pg),
        out_shape=jax.ShapeDtypeStruct((B, T, W), F32),
        grid_spec=pltpu.PrefetchScalarGridSpec(
            num_scalar_prefetch=1, grid=(B, n_pages // n_pg),
            in_specs=[row_spec, new_spec, new_spec] + [page_spec(p) for p in range(n_pg)] * 2,
            out_specs=row_spec,
            scratch_shapes=[pltpu.VMEM((rows, W), F32), pltpu.VMEM((rows, W), F32), pltpu.VMEM((rows, 1), F32)]),
        compiler_params=_cparams("parallel", "arbitrary"),
        name="sb_paged",
    )(page_table, q, k_new, v_new, *([cache_kt] * n_pg), *([cache_vt] * n_pg))


def _rope_lanes(x, cos, sin):
    W = x.shape[-1]
    lane = lax.broadcasted_iota(jnp.int32, x.shape, 1)
    first = (lane % MLA_ROPE) < ROPE_HALF
    from_right = pltpu.roll(x, W - ROPE_HALF, 1)
    from_left = pltpu.roll(x, ROPE_HALF, 1)
    return x * cos + jnp.where(first, -from_right, from_left) * sin


def _mla_prep_body(qd_ref, kvd_ref, kr_ref, cq_ref, sq_ref, ck_ref, sk_ref, qg_ref, wq_ref, kvg_ref, wuk_ref,
                   ql_ref, qp_ref, ckv_ref, kpe_ref, *, sb, tb):
    rows = sb * tb
    nope_w = MLA_HEADS * MLA_NOPE

    def table(ref):
        t = ref[...]
        return jnp.broadcast_to(t[None], (sb,) + t.shape).reshape(rows, t.shape[-1])

    qd = qd_ref[...]
    qn = qd * lax.rsqrt(jnp.mean(qd * qd, axis=-1, keepdims=True) + NORM_EPS) * qg_ref[...]
    q = _dot(qn.reshape(rows, MLA_Q_RANK).astype(BF16), wq_ref[...])
    q_pe = _rope_lanes(q[:, nope_w:], table(cq_ref), table(sq_ref))
    qp_ref[...] = q_pe.reshape(sb, tb, MLA_HEADS * MLA_ROPE)
    for h in range(MLA_HEADS):
        q_lat = _dot(q[:, h * MLA_NOPE:(h + 1) * MLA_NOPE].astype(BF16), wuk_ref[h])
        ql_ref[:, :, h * MLA_KV_RANK:(h + 1) * MLA_KV_RANK] = q_lat.reshape(sb, tb, MLA_KV_RANK)
    kvd = kvd_ref[...]
    ckv_ref[...] = kvd * lax.rsqrt(jnp.mean(kvd * kvd, axis=-1, keepdims=True) + NORM_EPS) * kvg_ref[...]
    kpe = _rope_lanes(kr_ref[...].reshape(rows, KR_PAD), table(ck_ref), table(sk_ref))
    kpe_ref[...] = kpe.reshape(sb, tb, KR_PAD)


def _mla_prep(qd, kvd, kr, tables, prm, rows=256):
    S, T, _ = qd.shape
    sb, tb = _blocking(S, T, rows)
    cos_q, sin_q, cos_k, sin_k = tables

    def row_spec(width):
        return pl.BlockSpec((sb, tb, width), lambda i, j: (i, j, 0))

    def tab_spec(width):
        return pl.BlockSpec((tb, width), lambda i, j: (j, 0))

    def full_spec(arr):
        return pl.BlockSpec(arr.shape, lambda i, j: (0,) * arr.ndim)

    widths = (MLA_HEADS * MLA_KV_RANK, MLA_HEADS * MLA_ROPE, MLA_KV_RANK, KR_PAD)
    return pl.pallas_call(
        functools.partial(_mla_prep_body, sb=sb, tb=tb),
        out_shape=[jax.ShapeDtypeStruct((S, T, n), F32) for n in widths],
        grid=(S // sb, T // tb),
        in_specs=[row_spec(MLA_Q_RANK), row_spec(MLA_KV_RANK), row_spec(KR_PAD),
                  tab_spec(cos_q.shape[1]), tab_spec(cos_q.shape[1]), tab_spec(KR_PAD), tab_spec(KR_PAD),
                  full_spec(prm["q_g"]), full_spec(prm["w_q"]), full_spec(prm["kv_g"]), full_spec(prm["w_uk"])],
        out_specs=[row_spec(n) for n in widths],
        compiler_params=_cparams("parallel", "parallel"),
        name="mla_prep",
    )(qd, kvd, kr, cos_q, sin_q, cos_k, sin_k, prm["q_g"], prm["w_q"], prm["kv_g"], prm["w_uk"])


def _softmax_step(s, c, m_s, l_s, acc_s):
    m_old = m_s[...]
    m_new = jnp.maximum(m_old, jnp.max(s, axis=-1, keepdims=True))
    alpha = jnp.exp(m_old - m_new)
    p = jnp.exp(s - m_new)
    l_s[...] = alpha * l_s[...] + jnp.sum(p, axis=-1, keepdims=True)
    acc_s[...] = alpha * acc_s[...] + _dot(p.astype(BF16), c)
    m_s[...] = m_new


def _stack_heads(ql_ref, qp_ref, ql_s, qp_s, T):
    for h in range(MLA_HEADS):
        ql_s[h * T:(h + 1) * T, :] = ql_ref[0, :, h * MLA_KV_RANK:(h + 1) * MLA_KV_RANK].astype(ql_s.dtype)
        qp_s[h * T:(h + 1) * T, :] = qp_ref[0, :, h * MLA_ROPE:(h + 1) * MLA_ROPE].astype(qp_s.dtype)


def _mla_finish(o_ref, wuv_ref, l_s, acc_s, T):
    o_lat = acc_s[...] / l_s[...]
    for h in range(MLA_HEADS):
        o_ref[0, :, h * MLA_V:(h + 1) * MLA_V] = _dot(o_lat[h * T:(h + 1) * T].astype(BF16), wuv_ref[h])


def _mla_prompt_body(ql_ref, qp_ref, ckv_ref, kpe_ref, wuv_ref, o_ref, ql_s, qp_s, m_s, l_s, acc_s):
    qi = pl.program_id(1)
    rows = MLA_HEADS * QBLK
    _stack_heads(ql_ref, qp_ref, ql_s, qp_s, QBLK)
    m_s[...] = jnp.full(m_s.shape, -jnp.inf, F32)
    l_s[...] = jnp.zeros(l_s.shape, F32)
    acc_s[...] = jnp.zeros(acc_s.shape, F32)
    q_pos = qi * QBLK + lax.broadcasted_iota(jnp.int32, (rows, MLA_KBLK), 0) % QBLK
    col = lax.broadcasted_iota(jnp.int32, (rows, MLA_KBLK), 1)

    def kb_body(kb, carry):
        off = pl.multiple_of(kb * MLA_KBLK, MLA_KBLK)
        c = ckv_ref[0, pl.ds(off, MLA_KBLK), :].astype(BF16)
        pe = kpe_ref[0, pl.ds(off, MLA_KBLK), 0:MLA_ROPE].astype(BF16)
        s = (_dot_nt(ql_s[...], c) + _dot_nt(qp_s[...], pe)) * MLA_SCALE
        s = jnp.where(off + col <= q_pos, s, -jnp.inf)
        _softmax_step(s, c, m_s, l_s, acc_s)
        return carry

    n_kb = (qi * QBLK + QBLK + MLA_KBLK - 1) // MLA_KBLK
    lax.fori_loop(0, n_kb, kb_body, 0)
    _mla_finish(o_ref, wuv_ref, l_s, acc_s, QBLK)


def _mla_prompt(q_lat, q_pe, ckv, kpe, w_uv):
    B, T, _ = q_lat.shape
    assert T % MLA_KBLK == 0
    rows = MLA_HEADS * QBLK
    return pl.pallas_call(
        _mla_prompt_body,
        out_shape=jax.ShapeDtypeStruct((B, T, MLA_WIDTH), F32),
        grid=(B, T // QBLK),
        in_specs=[pl.BlockSpec((1, QBLK, q_lat.shape[2]), lambda b, i: (b, i, 0)),
                  pl.BlockSpec((1, QBLK, q_pe.shape[2]), lambda b, i: (b, i, 0)),
                  pl.BlockSpec((1, T, MLA_KV_RANK), lambda b, i: (b, 0, 0)),
                  pl.BlockSpec((1, T, KR_PAD), lambda b, i: (b, 0, 0)),
                  pl.BlockSpec(w_uv.shape, lambda b, i: (0, 0, 0))],
        out_specs=pl.BlockSpec((1, QBLK, MLA_WIDTH), lambda b, i: (b, i, 0)),
        scratch_shapes=[pltpu.VMEM((rows, MLA_KV_RANK), BF16), pltpu.VMEM((rows, MLA_ROPE), BF16),
                        pltpu.VMEM((rows, 1), F32), pltpu.VMEM((rows, 1), F32),
                        pltpu.VMEM((rows, MLA_KV_RANK), F32)],
        compiler_params=_cparams("parallel", "arbitrary"),
        name="mla_prompt",
    )(q_lat, q_pe, ckv, kpe, w_uv)


def _mla_paged_body(pt_ref, ql_ref, qp_ref, cn_ref, pn_ref, *refs, n_pg):
    c_pages, p_pages = refs[:n_pg], refs[n_pg:2 * n_pg]
    wuv_ref, o_ref, ql_s, qp_s, m_s, l_s, acc_s = refs[2 * n_pg:]
    g = pl.program_id(1)
    T = ql_ref.shape[1]
    rows = MLA_HEADS * T

    @pl.when(g == 0)
    def _():
        _stack_heads(ql_ref, qp_ref, ql_s, qp_s, T)
        m_s[...] = jnp.full(m_s.shape, -jnp.inf, F32)
        l_s[...] = jnp.zeros(l_s.shape, F32)
        acc_s[...] = jnp.zeros(acc_s.shape, F32)
        t_pos = lax.broadcasted_iota(jnp.int32, (rows, PAGE_SIZE), 0) % T
        col = lax.broadcasted_iota(jnp.int32, (rows, PAGE_SIZE), 1)
        c = cn_ref[0].astype(BF16)
        pe = pn_ref[0, :, 0:MLA_ROPE].astype(BF16)
        s = (_dot_nt(ql_s[...].astype(BF16), c) + _dot_nt(qp_s[...].astype(BF16), pe)) * MLA_SCALE
        _softmax_step(jnp.where(col <= t_pos, s, -jnp.inf), c, m_s, l_s, acc_s)

    q_lat, q_pe = ql_s[...].astype(BF16), qp_s[...].astype(BF16)
    cs = [c_pages[p][...].astype(BF16) for p in range(n_pg)]
    ss = [(_dot_nt(q_lat, cs[p]) + _dot(q_pe, p_pages[p][...].astype(BF16))) * MLA_SCALE for p in range(n_pg)]
    m_old = m_s[...]
    m_new = m_old
    for s in ss:
        m_new = jnp.maximum(m_new, jnp.max(s, axis=-1, keepdims=True))
    alpha = jnp.exp(m_old - m_new)
    l = alpha * l_s[...]
    acc = alpha * acc_s[...]
    for p in range(n_pg):
        e = jnp.exp(ss[p] - m_new)
        l = l + jnp.sum(e, axis=-1, keepdims=True)
        acc = acc + _dot(e.astype(BF16), cs[p])
    m_s[...] = m_new
    l_s[...] = l
    acc_s[...] = acc

    @pl.when(g == pl.num_programs(1) - 1)
    def _():
        _mla_finish(o_ref, wuv_ref, l_s, acc_s, T)


def _mla_paged(q_lat, q_pe, ckv_new, kpe_new, cache_ckv, cache_kpe_t, page_table, j, w_uv):
    B, T, _ = q_lat.shape
    n_pages = page_table.shape[1]
    n_pg = PAGES_PER_STEP
    assert n_pages % n_pg == 0
    rows = MLA_HEADS * T

    def page_spec(shape, p):
        return pl.BlockSpec((None, None) + shape, lambda b, g, pt: (pt[b, g * n_pg + p], j, 0, 0))

    def seq_spec(t, width):
        return pl.BlockSpec((1, t, width), lambda b, g, pt: (b, 0, 0))

    return pl.pallas_call(
        functools.partial(_mla_paged_body, n_pg=n_pg),
        out_shape=jax.ShapeDtypeStruct((B, T, MLA_WIDTH), F32),
        grid_spec=pltpu.PrefetchScalarGridSpec(
            num_scalar_prefetch=1, grid=(B, n_pages // n_pg),
            in_specs=[seq_spec(T, q_lat.shape[2]), seq_spec(T, q_pe.shape[2]),
                      seq_spec(PAGE_SIZE, MLA_KV_RANK), seq_spec(PAGE_SIZE, KR_PAD)]
            + [page_spec((PAGE_SIZE, MLA_KV_RANK), p) for p in range(n_pg)]
            + [page_spec((MLA_ROPE, PAGE_SIZE), p) for p in range(n_pg)]
            + [pl.BlockSpec(w_uv.shape, lambda b, g, pt: (0, 0, 0))],
            out_specs=seq_spec(T, MLA_WIDTH),
            scratch_shapes=[pltpu.VMEM((rows, MLA_KV_RANK), F32), pltpu.VMEM((rows, MLA_ROPE), F32),
                            pltpu.VMEM((rows, 1), F32), pltpu.VMEM((rows, 1), F32),
                            pltpu.VMEM((rows, MLA_KV_RANK), F32)]),
        compiler_params=_cparams("parallel", "arbitrary"),
        name="mla_paged",
    )(page_table, q_lat, q_pe, ckv_new, kpe_new, *([cache_ckv] * n_pg), *([cache_kpe_t] * n_pg), w_uv)


S5_SEQS = 8


def _s5_body(u_ref, re0_ref, im0_ref, are_ref, aim_ref, bre_ref, bim_ref, cre_ref, cim_ref, d_ref, wglu_ref,
             bglu_ref, o_ref, re_out_ref, im_out_ref, xre_s, xim_s, re_s, im_s, *, tc):
    tj = pl.program_id(1)
    rows = tc * S5_SEQS

    @pl.when(tj == 0)
    def _():
        re_s[...] = re0_ref[...]
        im_s[...] = im0_ref[...]

    u = u_ref[...].reshape(rows, S5_WIDTH)
    ub = u.astype(BF16)
    xre_s[...] = _dot(ub, bre_ref[...]).reshape(tc, S5_SEQS, S5_FLAT)
    xim_s[...] = _dot(ub, bim_ref[...]).reshape(tc, S5_SEQS, S5_FLAT)
    a_re, a_im = are_ref[...], aim_ref[...]

    def step(t, carry):
        re, im = carry
        n_re = a_re * re - a_im * im + xre_s[t]
        n_im = a_re * im + a_im * re + xim_s[t]
        xre_s[t] = n_re
        xim_s[t] = n_im
        return n_re, n_im

    re, im = lax.fori_loop(0, tc, step, (re_s[...], im_s[...]))
    re_s[...] = re
    im_s[...] = im
    x_re = xre_s[...].reshape(rows, S5_FLAT).astype(BF16)
    x_im = xim_s[...].reshape(rows, S5_FLAT).astype(BF16)
    y = _dot(x_re, cre_ref[...]) - _dot(x_im, cim_ref[...]) + d_ref[...] * u
    y = 0.5 * y * (1.0 + lax.erf(y * (1.0 / math.sqrt(2.0))))
    out = y * jax.nn.sigmoid(_dot(y.astype(BF16), wglu_ref[...]) + bglu_ref[...])
    o_ref[...] = out.reshape(tc, S5_SEQS, S5_WIDTH)

    @pl.when(tj == pl.num_programs(1) - 1)
    def _():
        re_out_ref[...] = re
        im_out_ref[...] = im


def _s5(u_t, re0, im0, prm, tc):
    T, S, _ = u_t.shape
    assert S % S5_SEQS == 0 and T % tc == 0
    names = ("a_re", "a_im", "b_re", "b_im", "c_re", "c_im", "d", "w_glu", "b_glu")
    state_spec = pl.BlockSpec((S5_SEQS, S5_FLAT), lambda i, j: (i, 0))
    row_spec = pl.BlockSpec((tc, S5_SEQS, S5_WIDTH), lambda i, j: (j, i, 0))
    return pl.pallas_call(
        functools.partial(_s5_body, tc=tc),
        out_shape=[jax.ShapeDtypeStruct((T, S, S5_WIDTH), F32), jax.ShapeDtypeStruct((S, S5_FLAT), F32),
                   jax.ShapeDtypeStruct((S, S5_FLAT), F32)],
        grid=(S // S5_SEQS, T // tc),
        in_specs=[row_spec, state_spec, state_spec]
        + [pl.BlockSpec(prm[n].shape, lambda i, j: (0, 0)) for n in names],
        out_specs=[row_spec, state_spec, state_spec],
        scratch_shapes=[pltpu.VMEM((tc, S5_SEQS, S5_FLAT), F32)] * 2 + [pltpu.VMEM((S5_SEQS, S5_FLAT), F32)] * 2,
        compiler_params=_cparams("parallel", "arbitrary"),
        name="s5",
    )(u_t, re0, im0, *(prm[n] for n in names))


def _router_body(x_ref, g_ref, sc_ref, sh_ref, wr_ref, br_ref, h_ref, comb_ref, *, sb, tb):
    rows = sb * tb
    x = x_ref[...]
    h = x * lax.rsqrt(jnp.mean(x * x, axis=-1, keepdims=True) + NORM_EPS) * g_ref[...]
    h = h * (1.0 + sc_ref[...]) + sh_ref[...]
    h_ref[...] = h
    logits = jnp.dot(h.reshape(rows, D_MODEL), wr_ref[...], precision=HIGHEST,
                     preferred_element_type=F32) + br_ref[...]
    lane = lax.broadcasted_iota(jnp.int32, (rows, N_EXPERTS), 1)
    work = logits
    vals, sels = [], []
    for _ in range(TOP_K):
        m = jnp.max(work, axis=-1, keepdims=True)
        idx = jnp.min(jnp.where(work == m, lane, N_EXPERTS), axis=-1, keepdims=True)
        sel = lane == idx
        vals.append(m)
        sels.append(sel)
        work = jnp.where(sel, -jnp.inf, work)
    exps = [jnp.exp(v - vals[0]) for v in vals]
    denom = exps[0] + exps[1] + exps[2] + exps[3]
    comb = jnp.zeros((rows, N_EXPERTS), F32)
    for e, sel in zip(exps, sels):
        comb = jnp.where(sel, e / denom, comb)
    comb_ref[...] = comb.reshape(sb, tb, N_EXPERTS)


def _router(x, g, scale, shift, w_r, b_r, rows=512):
    S, T, D = x.shape
    sb, tb = _blocking(S, T, rows)
    row = pl.BlockSpec((sb, tb, D), lambda i, j: (i, j, 0))
    seq = pl.BlockSpec((sb, 1, D), lambda i, j: (i, 0, 0))
    return pl.pallas_call(
        functools.partial(_router_body, sb=sb, tb=tb),
        out_shape=[jax.ShapeDtypeStruct((S, T, D), F32), jax.ShapeDtypeStruct((S, T, N_EXPERTS), F32)],
        grid=(S // sb, T // tb),
        in_specs=[row, pl.BlockSpec((1, D), lambda i, j: (0, 0)), seq, seq,
                  pl.BlockSpec((D, N_EXPERTS), lambda i, j: (0, 0)),
                  pl.BlockSpec((1, N_EXPERTS), lambda i, j: (0, 0))],
        out_specs=[row, pl.BlockSpec((sb, tb, N_EXPERTS), lambda i, j: (i, j, 0))],
        compiler_params=_cparams("parallel", "parallel"),
        name="router",
    )(x, g.reshape(1, D), scale, shift, w_r, b_r.reshape(1, N_EXPERTS))


def _experts_body(h_ref, comb_ref, x_ref, gate_ref, wg_ref, bg_ref, wl_ref, bl_ref, wd_ref, bd_ref, o_ref,
                  hb_s, acc_s, *, sb, tb):
    e = pl.program_id(2)
    rows = sb * tb

    @pl.when(e == 0)
    def _():
        hb_s[...] = h_ref[...].reshape(rows, D_MODEL).astype(BF16)
        acc_s[...] = jnp.zeros(acc_s.shape, F32)

    hb = hb_s[...]
    glu = jnp.minimum(_dot(hb, wg_ref[0]) + bg_ref[0], SWIGLU_LIMIT)
    lin = jnp.clip(_dot(hb, wl_ref[0]) + bl_ref[0], -SWIGLU_LIMIT, SWIGLU_LIMIT)
    act = glu * jax.nn.sigmoid(SWIGLU_ALPHA * glu) * (lin + 1.0)
    y = _dot(act.astype(BF16), wd_ref[0]) + bd_ref[0]
    comb = comb_ref[...].reshape(rows, N_EXPERTS)
    lane = lax.broadcasted_iota(jnp.int32, (rows, N_EXPERTS), 1)
    w_e = jnp.sum(jnp.where(lane == e, comb, 0.0), axis=-1, keepdims=True)
    acc_s[...] = acc_s[...] + w_e * y

    @pl.when(e == pl.num_programs(2) - 1)
    def _():
        o_ref[...] = x_ref[...] + gate_ref[...] * acc_s[...].reshape(sb, tb, D_MODEL)


def _experts(h, comb, x, gate, prm, rows=1024):
    S, T, D = x.shape
    sb, tb = _blocking(S, T, rows)
    F = D_FF_EXPERT
    row = pl.BlockSpec((sb, tb, D), lambda i, j, e: (i, j, 0))
    return pl.pallas_call(
        functools.partial(_experts_body, sb=sb, tb=tb),
        out_shape=jax.ShapeDtypeStruct((S, T, D), F32),
        grid=(S // sb, T // tb, N_EXPERTS),
        in_specs=[row, pl.BlockSpec((sb, tb, N_EXPERTS), lambda i, j, e: (i, j, 0)), row,
                  pl.BlockSpec((sb, 1, D), lambda i, j, e: (i, 0, 0)),
                  pl.BlockSpec((1, D, F), lambda i, j, e: (e, 0, 0)),
                  pl.BlockSpec((1, 1, F), lambda i, j, e: (e, 0, 0)),
                  pl.BlockSpec((1, D, F), lambda i, j, e: (e, 0, 0)),
                  pl.BlockSpec((1, 1, F), lambda i, j, e: (e, 0, 0)),
                  pl.BlockSpec((1, F, D), lambda i, j, e: (e, 0, 0)),
                  pl.BlockSpec((1, 1, D), lambda i, j, e: (e, 0, 0))],
        out_specs=row,
        scratch_shapes=[pltpu.VMEM((sb * tb, D), BF16), pltpu.VMEM((sb * tb, D), F32)],
        compiler_params=_cparams("parallel", "parallel", "arbitrary"),
        name="experts",
    )(h, comb, x, gate, prm["w_g"], prm["b_g"], prm["w_l"], prm["b_l"], prm["w_d"], prm["b_d"])


def _rw_column_order():
    W, dl, al = RW_WIDTH, RW_DECAY_LORA, RW_AAA_LORA
    r = np.arange(0, W)
    wd = np.arange(W, W + dl)
    k = np.arange(W + dl, 2 * W + dl)
    v = np.arange(2 * W + dl, 3 * W + dl)
    rest = np.arange(3 * W + dl, RW_PROJ)
    return np.concatenate([r, k, v, wd, rest])


def _q_up_column_order():
    per = MLA_NOPE + MLA_ROPE
    nope = np.concatenate([np.arange(h * per, h * per + MLA_NOPE) for h in range(MLA_HEADS)])
    rope = np.concatenate([np.arange(h * per + MLA_NOPE, (h + 1) * per) for h in range(MLA_HEADS)])
    return np.concatenate([nope, rope])


def _block_diag(blocks):
    G, a, b = blocks.shape
    eye = jnp.eye(G, dtype=blocks.dtype)
    return (eye[:, None, :, None] * blocks[:, :, None, :]).reshape(G * a, G * b)


def _even_params(P, j):
    order = _rw_column_order()
    w_in = P["ev_w_in"][j]
    row = lambda a: a.reshape(1, -1).astype(F32)
    head = np.arange(RW_WIDTH) // RW_HEAD_DIM
    return dict(
        w_in=jnp.concatenate([w_in[:, :RW_PROJ][:, order], w_in[:, RW_PROJ:]], axis=1).astype(BF16),
        w_out_rw=P["ev_w_out"][j][:RW_WIDTH].astype(BF16),
        w_out_sb=P["ev_w_out"][j][RW_WIDTH:].astype(BF16),
        rw=dict(mu=row(P["rw_mu"][j][order]), w0=row(P["rw_w0"][j]), w2=P["rw_w2"][j].astype(BF16),
                a0=row(P["rw_a0"][j]), a2=P["rw_a2"][j].astype(BF16), g2=P["rw_g2"][j].astype(BF16),
                k_k=row(P["rw_k_k"][j]), k_a=row(P["rw_k_a"][j]), r_k=row(P["rw_r_k"][j]),
                ln_w=row(P["rw_ln_w"][j]), ln_b=row(P["rw_ln_b"][j]),
                ones=jnp.asarray((head[:, None] == head[None, :]).astype(np.float32))))


def _odd_params(P, j):
    row = lambda a: a.reshape(1, -1).astype(F32)
    w_in = P["od_w_in"][j]
    pad = jnp.zeros((D_MODEL, KR_PAD - MLA_ROPE), w_in.dtype)
    lr, li = P["s5_lam_re"][j].astype(F32), P["s5_lam_im"][j].astype(F32)
    dt = jnp.exp(P["s5_log_step"][j].astype(F32))[:, None]
    mag = jnp.exp(lr * dt)
    ab_re, ab_im = mag * jnp.cos(li * dt), mag * jnp.sin(li * dt)
    den = lr * lr + li * li
    f_re = ((ab_re - 1.0) * lr + ab_im * li) / den
    f_im = (ab_im * lr - (ab_re - 1.0) * li) / den
    b_re, b_im = P["s5_b_re"][j].astype(F32), P["s5_b_im"][j].astype(F32)
    bb_re = f_re[..., None] * b_re - f_im[..., None] * b_im
    bb_im = f_re[..., None] * b_im + f_im[..., None] * b_re
    to_in = lambda a: _block_diag(jnp.swapaxes(a, 1, 2)).astype(BF16)
    to_out = lambda a: _block_diag(jnp.swapaxes(a, 1, 2)).astype(BF16)
    return dict(
        w_in=jnp.concatenate([w_in, pad], axis=1).astype(BF16),
        w_out_s5=P["od_w_out"][j][:S5_WIDTH].astype(BF16),
        w_out_mla=P["od_w_out"][j][S5_WIDTH:].astype(BF16),
        s5=dict(a_re=row(ab_re), a_im=row(ab_im), b_re=to_in(bb_re), b_im=to_in(bb_im),
                c_re=to_out(P["s5_c_re"][j].astype(F32)), c_im=to_out(P["s5_c_im"][j].astype(F32)),
                d=row(P["s5_d"][j]), w_glu=P["s5_w_glu"][j].astype(BF16), b_glu=row(P["s5_b_glu"][j])),
        mla=dict(q_g=row(P["mla_q_norm_g"][j]), w_q=P["mla_w_q_up"][j][:, _q_up_column_order()].astype(BF16),
                 kv_g=row(P["mla_kv_norm_g"][j]),
                 w_uk=jnp.transpose(P["mla_w_uk"][j], (1, 2, 0)).astype(BF16),
                 w_uv=jnp.transpose(P["mla_w_uv"][j], (1, 0, 2)).astype(BF16)))


def _deinterleave_body(w_ref, sel_ref, even_ref, odd_ref):
    half = even_ref.shape[-1]
    y = _dot(w_ref[0].astype(BF16), sel_ref[...])
    even_ref[0] = y[:, :half].astype(BF16)
    odd_ref[0] = y[:, half:].astype(BF16)


def _deinterleave(w):
    E, D, F2 = w.shape
    F = F2 // 2
    col = np.arange(F2)
    sel = (np.arange(F2)[:, None] == (2 * (col % F) + col // F)[None, :]).astype(np.float32)
    return pl.pallas_call(
        _deinterleave_body,
        out_shape=[jax.ShapeDtypeStruct((E, D, F), BF16)] * 2,
        grid=(E,),
        in_specs=[pl.BlockSpec((1, D, F2), lambda e: (e, 0, 0)), pl.BlockSpec((F2, F2), lambda e: (0, 0))],
        out_specs=[pl.BlockSpec((1, D, F), lambda e: (e, 0, 0))] * 2,
        compiler_params=_cparams("parallel"),
        name="deinterleave",
    )(w, jnp.asarray(sel, BF16))


def _moe_params(P, l):
    w_gu, b_gu = P["moe_w_gu"][l], P["moe_b_gu"][l]
    w_g, w_l = _deinterleave(w_gu)
    return dict(w_r=P["moe_w_router"][l].astype(F32), b_r=P["moe_b_router"][l].astype(F32),
                w_g=w_g, w_l=w_l,
                b_g=b_gu[:, None, 0::2].astype(F32), b_l=b_gu[:, None, 1::2].astype(F32),
                w_d=P["moe_w_down"][l].astype(BF16), b_d=P["moe_b_down"][l][:, None, :].astype(F32))


def _rope_tables(q_pos):
    inv = jnp.power(ROPE_THETA, -jnp.arange(ROPE_HALF, dtype=F32) / ROPE_HALF)
    ang = q_pos.astype(F32)[:, None] * inv[None, :]
    cos, sin = jnp.cos(ang), jnp.sin(ang)
    q_rep = MLA_HEADS * MLA_ROPE // ROPE_HALF
    k_rep = KR_PAD // ROPE_HALF
    return jnp.tile(cos, (1, q_rep)), jnp.tile(sin, (1, q_rep)), jnp.tile(cos, (1, k_rep)), jnp.tile(sin, (1, k_rep))


def _pad_rows(a, n):
    return jnp.pad(a, ((0, 0), (0, n - a.shape[1]), (0, 0)))


def _trunk(x, c, q_pos, shift0, wkv0, s5re0, s5im0, paged, P, prepared):
    S, T, D = x.shape
    new = {name: [] for name in ("sb_k", "sb_v", "ckv", "kpe", "wkv", "shift", "s5re", "s5im")}
    tables = _rope_tables(q_pos)
    if paged is not None:
        paged = dict(paged, sb_k=jnp.transpose(paged["sb_k"], (0, 1, 3, 4, 2)),
                     sb_v=jnp.transpose(paged["sb_v"], (0, 1, 3, 4, 2)),
                     kpe=jnp.transpose(paged["kpe"], (0, 1, 3, 2)))
    rw_sb, rw_tb = (S5_SEQS, T) if T < 256 else (4, 256)
    for l in range(DEPTH):
        j = l // 2
        mod = _linear(c[None], P["w_mod"][l].astype(BF16), b=P["b_mod"][l], silu=True, tn=1536)[0]
        sh1, sc1, g1, sh2, sc2, g2 = (mod[0, :, i * D:(i + 1) * D][:, None, :] for i in range(6))
        if l % 2 == 0:
            ep = prepared["even"][j]
            rw, q, k, v = _linear(x, ep["w_in"], norm_g=P["norm_mix_g"][l], scale=sc1, shift=sh1,
                                  splits=[RW_PROJ, SB_WIDTH, SB_WIDTH, SB_WIDTH], rows=256)
            shift = _normmod(x[:, T - 1:T], P["norm_mix_g"][l], sc1, sh1)
            if shift0 is None:
                prev = jnp.zeros((S, 1, RW_PROJ), F32)
            else:
                prev = _linear(shift0[:, j][None], ep["w_in"][:, :RW_PROJ])[0].reshape(S, 1, RW_PROJ)
            rw_out, wkv = _rwkv(rw, prev, None if wkv0 is None else wkv0[:, j], ep["rw"], sb=rw_sb, tb=rw_tb)
            if paged is None:
                sb_out = _sb_prompt(q, k, v)
            else:
                sb_out = _sb_paged(q, _pad_rows(k, PAGE_SIZE), _pad_rows(v, PAGE_SIZE), paged["sb_k"],
                                   paged["sb_v"], paged["page_table"], j)
            x = _linear(rw_out, ep["w_out_rw"], x2=sb_out, w2=ep["w_out_sb"], res=x, gate=g1)[0]
            new["shift"].append(shift[:, 0])
            new["wkv"].append(wkv)
            new["sb_k"].append(k.reshape(S, T, SB_HEADS, SB_HEAD_DIM))
            new["sb_v"].append(v.reshape(S, T, SB_HEADS, SB_HEAD_DIM))
        else:
            op = prepared["odd"][j]
            u, qd, kvd, kr = _linear(x, op["w_in"], norm_g=P["norm_mix_g"][l], scale=sc1, shift=sh1,
                                     splits=[S5_WIDTH, MLA_Q_RANK, MLA_KV_RANK, KR_PAD], rows=256)
            re0 = jnp.zeros((S, S5_FLAT), F32) if s5re0 is None else s5re0[:, j].reshape(S, S5_FLAT)
            im0 = jnp.zeros((S, S5_FLAT), F32) if s5im0 is None else s5im0[:, j].reshape(S, S5_FLAT)
            s5_t, s_re, s_im = _s5(jnp.swapaxes(u, 0, 1), re0, im0, op["s5"], tc=min(T, 128))
            s5_out = jnp.swapaxes(s5_t, 0, 1)
            q_lat, q_pe, ckv, kpe = _mla_prep(qd, kvd, kr, tables, op["mla"])
            if paged is None:
                mla_out = _mla_prompt(q_lat, q_pe, ckv, kpe, op["mla"]["w_uv"])
            else:
                mla_out = _mla_paged(q_lat, q_pe, _pad_rows(ckv, PAGE_SIZE), _pad_rows(kpe, PAGE_SIZE),
                                     paged["ckv"], paged["kpe"], paged["page_table"], j, op["mla"]["w_uv"])
            x = _linear(s5_out, op["w_out_s5"], x2=mla_out, w2=op["w_out_mla"], res=x, gate=g1)[0]
            new["s5re"].append(s_re.reshape(S, S5_GROUPS, S5_STATE))
            new["s5im"].append(s_im.reshape(S, S5_GROUPS, S5_STATE))
            new["ckv"].append(ckv)
            new["kpe"].append(kpe[..., :MLA_ROPE])
        mp = prepared["moe"][l]
        h, comb = _router(x, P["norm_ffn_g"][l], sc2, sh2, mp["w_r"], mp["b_r"])
        x = _experts(h, comb, x, g2, mp)
    y = _normmod(x, P["norm_final_g"])
    return y, {name: jnp.stack(vals, axis=1) for name, vals in new.items()}


def kernel(x_prompt, x_sample, cache_sb_k, cache_sb_v, cache_mla_ckv, cache_mla_kpe, state_rwkv_wkv,
           state_rwkv_shift, state_s5_re, state_s5_im, page_table, c_prompt, c_sample, w_mod, b_mod,
           norm_mix_g, norm_ffn_g, norm_final_g, ev_w_in, ev_w_out, rw_mu, rw_w0, rw_w2, rw_a0, rw_a2, rw_g2,
           rw_k_k, rw_k_a, rw_r_k, rw_ln_w, rw_ln_b, od_w_in, od_w_out, s5_lam_re, s5_lam_im, s5_b_re, s5_b_im,
           s5_c_re, s5_c_im, s5_d, s5_log_step, s5_w_glu, s5_b_glu, mla_q_norm_g, mla_w_q_up, mla_kv_norm_g,
           mla_w_uk, mla_w_uv, moe_w_router, moe_b_router, moe_w_gu, moe_b_gu, moe_w_down, moe_b_down):
    P = dict(w_mod=w_mod, b_mod=b_mod, norm_mix_g=norm_mix_g, norm_ffn_g=norm_ffn_g, norm_final_g=norm_final_g,
             ev_w_in=ev_w_in, ev_w_out=ev_w_out, rw_mu=rw_mu, rw_w0=rw_w0, rw_w2=rw_w2, rw_a0=rw_a0, rw_a2=rw_a2,
             rw_g2=rw_g2, rw_k_k=rw_k_k, rw_k_a=rw_k_a, rw_r_k=rw_r_k, rw_ln_w=rw_ln_w, rw_ln_b=rw_ln_b,
             od_w_in=od_w_in, od_w_out=od_w_out, s5_lam_re=s5_lam_re, s5_lam_im=s5_lam_im, s5_b_re=s5_b_re,
             s5_b_im=s5_b_im, s5_c_re=s5_c_re, s5_c_im=s5_c_im, s5_d=s5_d, s5_log_step=s5_log_step,
             s5_w_glu=s5_w_glu, s5_b_glu=s5_b_glu, mla_q_norm_g=mla_q_norm_g, mla_w_q_up=mla_w_q_up,
             mla_kv_norm_g=mla_kv_norm_g, mla_w_uk=mla_w_uk, mla_w_uv=mla_w_uv, moe_w_router=moe_w_router,
             moe_b_router=moe_b_router, moe_w_gu=moe_w_gu, moe_b_gu=moe_b_gu, moe_w_down=moe_w_down,
             moe_b_down=moe_b_down)
    n_even, n_odd = (DEPTH + 1) // 2, DEPTH // 2
    prepared = dict(even=[_even_params(P, j) for j in range(n_even)],
                    odd=[_odd_params(P, j) for j in range(n_odd)],
                    moe=[_moe_params(P, l) for l in range(DEPTH)])
    sp = x_prompt.shape[1]
    y_prompt, pn = _trunk(x_prompt, c_prompt, jnp.arange(sp, dtype=jnp.int32), None, None, None, None, None,
                          P, prepared)
    past_len = page_table.shape[1] * cache_sb_k.shape[2]
    paged = dict(page_table=page_table, sb_k=cache_sb_k, sb_v=cache_sb_v, ckv=cache_mla_ckv, kpe=cache_mla_kpe)
    y_sample, sn = _trunk(x_sample, c_sample, past_len + jnp.arange(x_sample.shape[1], dtype=jnp.int32),
                          state_rwkv_shift, state_rwkv_wkv, state_s5_re, state_s5_im, paged, P, prepared)
    return (y_prompt, y_sample,
            pn["sb_k"], pn["sb_v"], pn["ckv"], pn["kpe"], pn["wkv"], pn["shift"], pn["s5re"], pn["s5im"],
            sn["sb_k"], sn["sb_v"], sn["ckv"], sn["kpe"], sn["wkv"], sn["shift"], sn["s5re"], sn["s5im"])
```

```python
import functools
import math

import numpy as np
import jax
import jax.numpy as jnp
from jax import lax
from jax.experimental import pallas as pl
from jax.experimental.pallas import tpu as pltpu

F32 = jnp.float32
BF16 = jnp.bfloat16
HIGHEST = lax.Precision.HIGHEST

D_MODEL = 1024
DEPTH = 4
PAGE_SIZE = 128
NORM_EPS = 1e-6

RW_HEADS = 8
RW_HEAD_DIM = 64
RW_WIDTH = RW_HEADS * RW_HEAD_DIM
RW_DECAY_LORA = 64
RW_AAA_LORA = 64
RW_GATE_LORA = 128
RW_PROJ = 3 * RW_WIDTH + RW_DECAY_LORA + RW_AAA_LORA + RW_GATE_LORA
RW_GN_EPS = 64e-5

SB_HEADS = 8
SB_HEAD_DIM = 64
SB_WIDTH = SB_HEADS * SB_HEAD_DIM
SB_SCALE = SB_HEAD_DIM ** -0.5

S5_GROUPS = 32
S5_GROUP_CH = 16
S5_WIDTH = S5_GROUPS * S5_GROUP_CH
S5_STATE = 64
S5_FLAT = S5_GROUPS * S5_STATE

MLA_HEADS = 8
MLA_Q_RANK = 384
MLA_KV_RANK = 256
MLA_NOPE = 64
MLA_ROPE = 32
MLA_V = 64
MLA_WIDTH = MLA_HEADS * MLA_V
MLA_SCALE = (MLA_NOPE + MLA_ROPE) ** -0.5
ROPE_THETA = 10000.0
ROPE_HALF = MLA_ROPE // 2
KR_PAD = 128

N_EXPERTS = 32
TOP_K = 4
D_FF_EXPERT = 512
SWIGLU_LIMIT = 7.0
SWIGLU_ALPHA = 1.702

VMEM_LIMIT_BYTES = 56 * 1024 * 1024
QBLK = 128
SB_QBLK = 512
SB_KBLK = 256
SB_HEAD_GROUP = 4
MLA_KBLK = 512
PAGES_PER_STEP = 16


def _cparams(*sem):
    return pltpu.CompilerParams(dimension_semantics=sem, vmem_limit_bytes=VMEM_LIMIT_BYTES)


def _blocking(S, T, rows):
    if T >= rows:
        assert T % rows == 0
        return 1, rows
    sb = max(1, min(S, rows // T))
    assert S % sb == 0
    return sb, T


def _softplus(x):
    return jnp.maximum(x, 0.0) + jnp.log1p(jnp.exp(-jnp.abs(x)))


def _dot(a, b):
    return jnp.dot(a, b, preferred_element_type=F32)


def _dot_nt(a, b):
    return lax.dot_general(a, b, (((1,), (1,)), ((), ())), preferred_element_type=F32)


def _split_hi_lo(x):
    hi = x.astype(BF16)
    lo = (x - hi.astype(F32)).astype(BF16)
    return hi, lo


def _linear_body(*refs, n_out, splits, has_norm, has_mod, silu, has_x2, has_b, has_res, sb, tb):
    outs = refs[len(refs) - n_out:]
    it = iter(refs[:len(refs) - n_out])
    x = next(it)[...]
    if has_norm:
        g_ref = next(it)
        x = x * lax.rsqrt(jnp.mean(x * x, axis=-1, keepdims=True) + NORM_EPS) * g_ref[...]
    if has_mod:
        sc_ref, sh_ref = next(it), next(it)
        x = x * (1.0 + sc_ref[...]) + sh_ref[...]
    if silu:
        x = x * jax.nn.sigmoid(x)
    w_ref = next(it)
    rows = sb * tb
    y = _dot(x.reshape(rows, x.shape[-1]).astype(BF16), w_ref[...])
    if has_x2:
        x2_ref, w2_ref = next(it), next(it)
        x2 = x2_ref[...]
        y = y + _dot(x2.reshape(rows, x2.shape[-1]).astype(BF16), w2_ref[...])
    if has_b:
        y = y + next(it)[...]
    if has_res:
        res_ref, gate_ref = next(it), next(it)
    off = 0
    for o_ref, n in zip(outs, splits):
        piece = y[:, off:off + n].reshape(sb, tb, n)
        if has_res:
            piece = res_ref[...] + gate_ref[...] * piece
        o_ref[...] = piece
        off += n


def _linear(x, w, *, b=None, norm_g=None, scale=None, shift=None, silu=False, x2=None, w2=None,
            res=None, gate=None, splits=None, rows=512, tn=None):
    S, T, K = x.shape
    N = w.shape[1]
    splits = list(splits) if splits is not None else [N]
    tn = N if tn is None else tn
    assert N % tn == 0 and (tn == N or len(splits) == 1)
    assert res is None or len(splits) == 1
    sb, tb = _blocking(S, T, rows)
    grid = (S // sb, T // tb, N // tn)

    def row_spec(width):
        return pl.BlockSpec((sb, tb, width), lambda i, j, n: (i, j, 0))

    def seq_spec(width):
        return pl.BlockSpec((sb, 1, width), lambda i, j, n: (i, 0, 0))

    args, specs = [x], [row_spec(K)]
    if norm_g is not None:
        args.append(norm_g.reshape(1, K))
        specs.append(pl.BlockSpec((1, K), lambda i, j, n: (0, 0)))
    if scale is not None:
        args += [scale, shift]
        specs += [seq_spec(K), seq_spec(K)]
    args.append(w)
    specs.append(pl.BlockSpec((K, tn), lambda i, j, n: (0, n)))
    if x2 is not None:
        K2 = x2.shape[-1]
        args += [x2, w2]
        specs += [row_spec(K2), pl.BlockSpec((K2, tn), lambda i, j, n: (0, n))]
    if b is not None:
        args.append(b.reshape(1, N))
        specs.append(pl.BlockSpec((1, tn), lambda i, j, n: (0, n)))
    if res is not None:
        args += [res, gate]
        specs += [pl.BlockSpec((sb, tb, tn), lambda i, j, n: (i, j, n)),
                  pl.BlockSpec((sb, 1, tn), lambda i, j, n: (i, 0, n))]
    if tn == N:
        out_specs = [row_spec(n) for n in splits]
        blk_splits = splits
    else:
        out_specs = [pl.BlockSpec((sb, tb, tn), lambda i, j, n: (i, j, n))]
        blk_splits = [tn]
    body = functools.partial(
        _linear_body, n_out=len(splits), splits=blk_splits, has_norm=norm_g is not None,
        has_mod=scale is not None, silu=silu, has_x2=x2 is not None, has_b=b is not None,
        has_res=res is not None, sb=sb, tb=tb)
    return pl.pallas_call(
        body,
        out_shape=[jax.ShapeDtypeStruct((S, T, n), F32) for n in splits],
        grid=grid, in_specs=specs, out_specs=out_specs,
        compiler_params=_cparams("parallel", "parallel", "arbitrary"),
        name="linear",
    )(*args)


def _normmod_body(*refs, has_mod):
    it = iter(refs)
    x = next(it)[...]
    g_ref = next(it)
    y = x * lax.rsqrt(jnp.mean(x * x, axis=-1, keepdims=True) + NORM_EPS) * g_ref[...]
    if has_mod:
        sc_ref, sh_ref = next(it), next(it)
        y = y * (1.0 + sc_ref[...]) + sh_ref[...]
    next(it)[...] = y


def _normmod(x, g, scale=None, shift=None, rows=512):
    S, T, K = x.shape
    sb, tb = _blocking(S, T, rows)
    args = [x, g.reshape(1, K)]
    specs = [pl.BlockSpec((sb, tb, K), lambda i, j: (i, j, 0)), pl.BlockSpec((1, K), lambda i, j: (0, 0))]
    if scale is not None:
        args += [scale, shift]
        specs += [pl.BlockSpec((sb, 1, K), lambda i, j: (i, 0, 0))] * 2
    return pl.pallas_call(
        functools.partial(_normmod_body, has_mod=scale is not None),
        out_shape=jax.ShapeDtypeStruct((S, T, K), F32),
        grid=(S // sb, T // tb), in_specs=specs,
        out_specs=pl.BlockSpec((sb, tb, K), lambda i, j: (i, j, 0)),
        compiler_params=_cparams("parallel", "parallel"),
        name="normmod",
    )(*args)


RW_COLS = RW_PROJ


def _rwkv_body(*refs, has_state, sb, tb, chunk):
    it = iter(refs)
    rw_ref, prev_ref = next(it), next(it)
    wkv0_ref = next(it) if has_state else None
    (mu_ref, w0_ref, w2_ref, a0_ref, a2_ref, g2_ref, kk_w_ref, ka_ref, rk_ref, lnw_ref,
     lnb_ref) = (next(it) for _ in range(11))
    out_ref, wkv_ref = next(it), next(it)
    carry_s, state_s, r_s, w_s, k_s, v_s, kk_s, b_s, y_s = (next(it) for _ in range(9))
    H, N, W = RW_HEADS, RW_HEAD_DIM, RW_WIDTH
    HALF = W // 2
    rows = sb * tb
    tj = pl.program_id(1)

    @pl.when(tj == 0)
    def _():
        carry_s[...] = prev_ref[...]
        if has_state:
            for h in range(H):
                state_s[:, :, h * N:(h + 1) * N] = wkv0_ref[:, h]
        else:
            state_s[...] = jnp.zeros(state_s.shape, F32)

    x = rw_ref[...]
    rolled = pltpu.roll(x.reshape(rows, RW_COLS), 1, 0).reshape(sb, tb, RW_COLS)
    tpos = lax.broadcasted_iota(jnp.int32, (sb, tb, 1), 1)
    shifted = jnp.where(tpos == 0, carry_s[...], rolled)
    carry_s[...] = x[:, tb - 1:tb, :]
    xm = (x + (shifted - x) * mu_ref[...]).reshape(rows, RW_COLS)
    r, k, v = xm[:, 0:W], xm[:, W:2 * W], xm[:, 2 * W:3 * W]
    o = 3 * W
    wd = xm[:, o:o + RW_DECAY_LORA]
    ad = xm[:, o + RW_DECAY_LORA:o + RW_DECAY_LORA + RW_AAA_LORA]
    gd = xm[:, o + RW_DECAY_LORA + RW_AAA_LORA:RW_COLS]

    w_log = -_softplus(-(w0_ref[...] + _dot(jnp.tanh(wd).astype(BF16), w2_ref[...]))) - 0.5
    decay = jnp.exp(-jnp.exp(w_log))
    a = jax.nn.sigmoid(a0_ref[...] + _dot(ad.astype(BF16), a2_ref[...]))
    g = _dot(jax.nn.sigmoid(gd).astype(BF16), g2_ref[...])
    pr = lax.broadcasted_iota(jnp.int32, (HALF, HALF), 0) // N
    pc = lax.broadcasted_iota(jnp.int32, (HALF, HALF), 1) // N
    head_ones = (pr == pc).astype(BF16)

    def head_sums(xs, terms=2):
        m = xs[0].shape[0]
        parts = []
        for xv in xs:
            rest = xv
            for _ in range(terms):
                part = rest.astype(BF16)
                parts.append(part)
                rest = rest - part.astype(F32)
        stacked = jnp.concatenate(parts, axis=0)
        res = jnp.concatenate([_dot(stacked[:, c:c + HALF], head_ones) for c in (0, HALF)], axis=-1)
        return [sum(res[(terms * i + t) * m:(terms * i + t + 1) * m] for t in range(terms)) for i in range(len(xs))]

    kk = k * kk_w_ref[...]
    kk_sq, = head_sums([kk * kk], terms=3)
    kk = kk / jnp.maximum(jnp.sqrt(kk_sq), 1e-12)
    k = k * (1.0 + (a - 1.0) * ka_ref[...])
    for dst, src in ((r_s, r), (w_s, decay), (k_s, k), (v_s, v), (kk_s, kk), (b_s, kk * a)):
        dst[...] = src.reshape(sb, tb, W)

    lane = lax.broadcasted_iota(jnp.int32, (N, W), 1)
    sub = lax.broadcasted_iota(jnp.int32, (N, W), 0)
    diag = (lane % N == sub).astype(F32)
    m_state = sb * N

    def state_sums(xs):
        return [r_.reshape(sb, N, W) for r_ in head_sums([xv.reshape(m_state, W) for xv in xs])]

    def chunk_body(c, carry):
        t0 = pl.multiple_of(c * chunk, chunk)
        y_rows = []
        st = state_s[...]
        for i in range(chunk):
            ts = pl.ds(t0 + i, 1)
            s_kk, v_col = state_sums([st * kk_s[:, ts, :], diag * v_s[:, ts, :]])
            st = st * w_s[:, ts, :] - s_kk * b_s[:, ts, :] + v_col * k_s[:, ts, :]
            y_b, = state_sums([st * r_s[:, ts, :]])
            y_rows.append(jnp.sum(y_b * diag, axis=1, keepdims=True))
        state_s[...] = st
        y_s[:, pl.ds(t0, chunk), :] = jnp.concatenate(y_rows, axis=1)
        return carry

    lax.fori_loop(0, tb // chunk, chunk_body, 0)

    y = y_s[...].reshape(rows, W)
    y_sum, rk_sum = head_sums([y, r * k * rk_ref[...]], terms=3)
    dev = y - y_sum * (1.0 / N)
    var, = head_sums([dev * dev], terms=3)
    yn = dev * lax.rsqrt(var * (1.0 / N) + RW_GN_EPS) * lnw_ref[...] + lnb_ref[...]
    bonus = rk_sum * v
    out_ref[...] = ((yn + bonus) * g).reshape(sb, tb, W)

    @pl.when(tj == pl.num_programs(1) - 1)
    def _():
        for h in range(H):
            wkv_ref[:, h] = state_s[:, :, h * N:(h + 1) * N]


def _rwkv(rw, prev, wkv0, prm, *, sb, tb):
    S, T, _ = rw.shape
    H, N, W = RW_HEADS, RW_HEAD_DIM, RW_WIDTH
    chunk = min(tb, 16)
    has_state = wkv0 is not None
    args = [rw, prev]
    specs = [pl.BlockSpec((sb, tb, RW_COLS), lambda i, j: (i, j, 0)),
             pl.BlockSpec((sb, 1, RW_COLS), lambda i, j: (i, 0, 0))]
    if has_state:
        args.append(wkv0)
        specs.append(pl.BlockSpec((sb, H, N, N), lambda i, j: (i, 0, 0, 0)))
    for name in ("mu", "w0", "w2", "a0", "a2", "g2", "k_k", "k_a", "r_k", "ln_w", "ln_b"):
        arr = prm[name]
        args.append(arr)
        specs.append(pl.BlockSpec(arr.shape, lambda i, j: (0, 0)))
    tok_scratch = pltpu.VMEM((sb, tb, W), F32)
    return pl.pallas_call(
        functools.partial(_rwkv_body, has_state=has_state, sb=sb, tb=tb, chunk=chunk),
        out_shape=[jax.ShapeDtypeStruct((S, T, W), F32), jax.ShapeDtypeStruct((S, H, N, N), F32)],
        grid=(S // sb, T // tb), in_specs=specs,
        out_specs=[pl.BlockSpec((sb, tb, W), lambda i, j: (i, j, 0)),
                   pl.BlockSpec((sb, H, N, N), lambda i, j: (i, 0, 0, 0))],
        scratch_shapes=[pltpu.VMEM((sb, 1, RW_COLS), F32), pltpu.VMEM((sb, N, W), F32)]
        + [tok_scratch] * 7,
        compiler_params=_cparams("parallel", "arbitrary"),
        name="rwkv7",
    )(*args)


def _sb_scores(z, valid):
    log_beta = -_softplus(-z)
    log_rem = log_beta - z
    if valid is not None:
        log_rem = jnp.where(valid, log_rem, 0.0)
    return log_beta, log_rem


def _sb_weights(log_beta, log_rem, valid, tri, run):
    hi, lo = _split_hi_lo(log_rem)
    after = _dot(hi, tri) + _dot(lo, tri)
    a = jnp.exp(log_beta + after + run)
    if valid is not None:
        a = jnp.where(valid, a, 0.0)
    return a.astype(BF16), run + jnp.sum(log_rem, axis=-1, keepdims=True)


def _later_key_matrix(n):
    j = lax.broadcasted_iota(jnp.int32, (n, n), 0)
    s = lax.broadcasted_iota(jnp.int32, (n, n), 1)
    return (j > s).astype(BF16)


def _pair_later_matrix():
    n = 2 * PAGE_SIZE
    def pos(c):
        return (1 - c // PAGE_SIZE) * PAGE_SIZE + c % PAGE_SIZE
    j = pos(lax.broadcasted_iota(jnp.int32, (n, n), 0))
    s = pos(lax.broadcasted_iota(jnp.int32, (n, n), 1))
    return (j > s).astype(BF16)


def _later_sums(log_rems, tri):
    m = log_rems[0].shape[0]
    parts = []
    for lr in log_rems:
        parts += list(_split_hi_lo(lr))
    res = _dot(jnp.concatenate(parts, axis=0), tri)
    return [res[2 * i * m:(2 * i + 1) * m] + res[(2 * i + 1) * m:(2 * i + 2) * m] for i in range(len(log_rems))]


def _sb_prompt_body(q_ref, k_ref, v_ref, o_ref, acc_s, run_s):
    qi = pl.program_id(1)
    D = SB_HEAD_DIM
    row = lax.broadcasted_iota(jnp.int32, (SB_QBLK, SB_KBLK), 0)
    col = lax.broadcasted_iota(jnp.int32, (SB_QBLK, SB_KBLK), 1)
    tri = _later_key_matrix(SB_KBLK)
    acc_s[...] = jnp.zeros(acc_s.shape, F32)
    run_s[...] = jnp.zeros(run_s.shape, F32)
    n_kb = (qi + 1) * (SB_QBLK // SB_KBLK)

    def kb_body(i, carry):
        off = pl.multiple_of((n_kb - 1 - i) * SB_KBLK, SB_KBLK)
        valid = (off + col) < (qi * SB_QBLK + row)
        for h0 in range(0, SB_HEADS, SB_HEAD_GROUP):
            heads = range(h0, h0 + SB_HEAD_GROUP)
            scores = []
            for h in heads:
                hs = slice(h * D, (h + 1) * D)
                qh = (q_ref[0, :, hs] * SB_SCALE).astype(BF16)
                kh = k_ref[0, pl.ds(off, SB_KBLK), hs].astype(BF16)
                scores.append(_sb_scores(_dot_nt(qh, kh), valid))
            afters = _later_sums([sc[1] for sc in scores], tri)
            for h, (log_beta, log_rem), after in zip(heads, scores, afters):
                hs = slice(h * D, (h + 1) * D)
                a = jnp.where(valid, jnp.exp(log_beta + after + run_s[h]), 0.0).astype(BF16)
                acc_s[h] = acc_s[h] + _dot(a, v_ref[0, pl.ds(off, SB_KBLK), hs].astype(BF16))
                run_s[h] = run_s[h] + jnp.sum(log_rem, axis=-1, keepdims=True)
        return carry

    lax.fori_loop(0, n_kb, kb_body, 0)
    for h in range(SB_HEADS):
        o_ref[0, :, h * D:(h + 1) * D] = acc_s[h]


def _sb_prompt(q, k, v):
    B, T, W = q.shape
    assert T % SB_QBLK == 0
    return pl.pallas_call(
        _sb_prompt_body,
        out_shape=jax.ShapeDtypeStruct((B, T, W), F32),
        grid=(B, T // SB_QBLK),
        in_specs=[pl.BlockSpec((1, SB_QBLK, W), lambda b, i: (b, i, 0)),
                  pl.BlockSpec((1, T, W), lambda b, i: (b, 0, 0)),
                  pl.BlockSpec((1, T, W), lambda b, i: (b, 0, 0))],
        out_specs=pl.BlockSpec((1, SB_QBLK, W), lambda b, i: (b, i, 0)),
        scratch_shapes=[pltpu.VMEM((SB_HEADS, SB_QBLK, SB_HEAD_DIM), F32), pltpu.VMEM((SB_HEADS, SB_QBLK, 1), F32)],
        compiler_params=_cparams("parallel", "arbitrary"),
        name="sb_prompt",
    )(q, k, v)


def _sb_paged_body(pt_ref, q_ref, kn_ref, vn_ref, *refs, n_pg):
    k_pages, v_pages = refs[:n_pg], refs[n_pg:2 * n_pg]
    o_ref, q_s, acc_s, run_s = refs[2 * n_pg:]
    g = pl.program_id(1)
    D, T, W = SB_HEAD_DIM, q_ref.shape[1], SB_WIDTH
    rows = SB_HEADS * T
    assert n_pg % 2 == 0

    @pl.when(g == 0)
    def _():
        tri = _later_key_matrix(PAGE_SIZE)
        q_s[...] = jnp.zeros(q_s.shape, F32)
        for h in range(SB_HEADS):
            q_s[h * T:(h + 1) * T, h * D:(h + 1) * D] = q_ref[0, :, h * D:(h + 1) * D] * SB_SCALE
        t_pos = lax.broadcasted_iota(jnp.int32, (rows, PAGE_SIZE), 0) % T
        col = lax.broadcasted_iota(jnp.int32, (rows, PAGE_SIZE), 1)
        valid = col < t_pos
        log_beta, log_rem = _sb_scores(_dot_nt(q_s[...].astype(BF16), kn_ref[0].astype(BF16)), valid)
        a, run = _sb_weights(log_beta, log_rem, valid, tri, jnp.zeros((rows, 1), F32))
        acc_s[...] = _dot(a, vn_ref[0].astype(BF16))
        run_s[...] = run

    qbd = q_s[...].astype(BF16)
    k_all = jnp.concatenate([k_pages[p][...].reshape(W, PAGE_SIZE) for p in range(n_pg)], axis=1).astype(BF16)
    v_all = jnp.concatenate([v_pages[p][...].reshape(W, PAGE_SIZE) for p in range(n_pg)], axis=1).astype(BF16)
    log_beta, log_rem = _sb_scores(_dot(qbd, k_all), None)
    pair = 2 * PAGE_SIZE
    n_pairs = n_pg // 2
    afters = _later_sums([log_rem[:, i * pair:(i + 1) * pair] for i in range(n_pairs)], _pair_later_matrix())
    run = run_s[...]
    weights = []
    for i in range(n_pairs):
        weights.append(jnp.exp(log_beta[:, i * pair:(i + 1) * pair] + afters[i] + run).astype(BF16))
        run = run + jnp.sum(log_rem[:, i * pair:(i + 1) * pair], axis=-1, keepdims=True)
    acc = acc_s[...] + _dot_nt(jnp.concatenate(weights, axis=1), v_all)
    acc_s[...] = acc
    run_s[...] = run

    @pl.when(g == pl.num_programs(1) - 1)
    def _():
        for h in range(SB_HEADS):
            o_ref[0, :, h * D:(h + 1) * D] = acc_s[h * T:(h + 1) * T, h * D:(h + 1) * D]


def _sb_paged(q, k_new, v_new, cache_kt, cache_vt, page_table, j):
    B, T, W = q.shape
    n_pages = page_table.shape[1]
    n_pg = PAGES_PER_STEP
    assert n_pages % n_pg == 0
    rows = SB_HEADS * T

    def page_spec(p):
        return pl.BlockSpec((None, None, SB_HEADS, SB_HEAD_DIM, PAGE_SIZE),
                            lambda b, g, pt: (pt[b, n_pages - 1 - (g * n_pg + p)], j, 0, 0, 0))

    new_spec = pl.BlockSpec((1, PAGE_SIZE, W), lambda b, g, pt: (b, 0, 0))
    row_spec = pl.BlockSpec((1, T, W), lambda b, g, pt: (b, 0, 0))
    return pl.pallas_call(
        functools.partial(_sb_paged_body, n_pg=n_pg),
        out_shape=jax.ShapeDtypeStruct((B, T, W), F32),
        grid_spec=pltpu.PrefetchScalarGridSpec(
            num_scalar_prefetch=1, grid=(B, n_pages // n_pg),
            in_specs=[row_spec, new_spec, new_spec] + [page_spec(p) for p in range(n_pg)] * 2,
            out_specs=row_spec,
            scratch_shapes=[pltpu.VMEM((rows, W), F32), pltpu.VMEM((rows, W), F32), pltpu.VMEM((rows, 1), F32)]),
        compiler_params=_cparams("parallel", "arbitrary"),
        name="sb_paged",
    )(page_table, q, k_new, v_new, *([cache_kt] * n_pg), *([cache_vt] * n_pg))


def _rope_lanes(x, cos, sin):
    W = x.shape[-1]
    lane = lax.broadcasted_iota(jnp.int32, x.shape, 1)
    first = (lane % MLA_ROPE) < ROPE_HALF
    from_right = pltpu.roll(x, W - ROPE_HALF, 1)
    from_left = pltpu.roll(x, ROPE_HALF, 1)
    return x * cos + jnp.where(first, -from_right, from_left) * sin


def _mla_prep_body(qd_ref, kvd_ref, kr_ref, cq_ref, sq_ref, ck_ref, sk_ref, qg_ref, wq_ref, kvg_ref, wuk_ref,
                   ql_ref, qp_ref, ckv_ref, kpe_ref, *, sb, tb):
    rows = sb * tb
    nope_w = MLA_HEADS * MLA_NOPE

    def table(ref):
        t = ref[...]
        return jnp.broadcast_to(t[None], (sb,) + t.shape).reshape(rows, t.shape[-1])

    qd = qd_ref[...]
    qn = qd * lax.rsqrt(jnp.mean(qd * qd, axis=-1, keepdims=True) + NORM_EPS) * qg_ref[...]
    q = _dot(qn.reshape(rows, MLA_Q_RANK).astype(BF16), wq_ref[...])
    q_pe = _rope_lanes(q[:, nope_w:], table(cq_ref), table(sq_ref))
    qp_ref[...] = q_pe.reshape(sb, tb, MLA_HEADS * MLA_ROPE)
    for h in range(MLA_HEADS):
        q_lat = _dot(q[:, h * MLA_NOPE:(h + 1) * MLA_NOPE].astype(BF16), wuk_ref[h])
        ql_ref[:, :, h * MLA_KV_RANK:(h + 1) * MLA_KV_RANK] = q_lat.reshape(sb, tb, MLA_KV_RANK)
    kvd = kvd_ref[...]
    ckv_ref[...] = kvd * lax.rsqrt(jnp.mean(kvd * kvd, axis=-1, keepdims=True) + NORM_EPS) * kvg_ref[...]
    kpe = _rope_lanes(kr_ref[...].reshape(rows, KR_PAD), table(ck_ref), table(sk_ref))
    kpe_ref[...] = kpe.reshape(sb, tb, KR_PAD)


def _mla_prep(qd, kvd, kr, tables, prm, rows=256):
    S, T, _ = qd.shape
    sb, tb = _blocking(S, T, rows)
    cos_q, sin_q, cos_k, sin_k = tables

    def row_spec(width):
        return pl.BlockSpec((sb, tb, width), lambda i, j: (i, j, 0))

    def tab_spec(width):
        return pl.BlockSpec((tb, width), lambda i, j: (j, 0))

    def full_spec(arr):
        return pl.BlockSpec(arr.shape, lambda i, j: (0,) * arr.ndim)

    widths = (MLA_HEADS * MLA_KV_RANK, MLA_HEADS * MLA_ROPE, MLA_KV_RANK, KR_PAD)
    return pl.pallas_call(
        functools.partial(_mla_prep_body, sb=sb, tb=tb),
        out_shape=[jax.ShapeDtypeStruct((S, T, n), F32) for n in widths],
        grid=(S // sb, T // tb),
        in_specs=[row_spec(MLA_Q_RANK), row_spec(MLA_KV_RANK), row_spec(KR_PAD),
                  tab_spec(cos_q.shape[1]), tab_spec(cos_q.shape[1]), tab_spec(KR_PAD), tab_spec(KR_PAD),
                  full_spec(prm["q_g"]), full_spec(prm["w_q"]), full_spec(prm["kv_g"]), full_spec(prm["w_uk"])],
        out_specs=[row_spec(n) for n in widths],
        compiler_params=_cparams("parallel", "parallel"),
        name="mla_prep",
    )(qd, kvd, kr, cos_q, sin_q, cos_k, sin_k, prm["q_g"], prm["w_q"], prm["kv_g"], prm["w_uk"])


def _softmax_step(s, c, m_s, l_s, acc_s):
    m_old = m_s[...]
    m_new = jnp.maximum(m_old, jnp.max(s, axis=-1, keepdims=True))
    alpha = jnp.exp(m_old - m_new)
    p = jnp.exp(s - m_new)
    l_s[...] = alpha * l_s[...] + jnp.sum(p, axis=-1, keepdims=True)
    acc_s[...] = alpha * acc_s[...] + _dot(p.astype(BF16), c)
    m_s[...] = m_new


def _stack_heads(ql_ref, qp_ref, ql_s, qp_s, T):
    for h in range(MLA_HEADS):
        ql_s[h * T:(h + 1) * T, :] = ql_ref[0, :, h * MLA_KV_RANK:(h + 1) * MLA_KV_RANK].astype(ql_s.dtype)
        qp_s[h * T:(h + 1) * T, :] = qp_ref[0, :, h * MLA_ROPE:(h + 1) * MLA_ROPE].astype(qp_s.dtype)


def _mla_finish(o_ref, wuv_ref, l_s, acc_s, T):
    o_lat = acc_s[...] / l_s[...]
    for h in range(MLA_HEADS):
        o_ref[0, :, h * MLA_V:(h + 1) * MLA_V] = _dot(o_lat[h * T:(h + 1) * T].astype(BF16), wuv_ref[h])


MLA_ROW_GROUPS = 4


def _mla_prompt_body(ql_ref, qp_ref, ckv_ref, kpe_ref, wuv_ref, o_ref, ql_s, qp_s, m_s, l_s, acc_s):
    qi = pl.program_id(1)
    G = MLA_ROW_GROUPS
    hpg = MLA_HEADS // G
    grows = hpg * QBLK
    for h in range(MLA_HEADS):
        g, r0 = h // hpg, (h % hpg) * QBLK
        ql_s[g, r0:r0 + QBLK, :] = ql_ref[0, :, h * MLA_KV_RANK:(h + 1) * MLA_KV_RANK].astype(BF16)
        qp_s[g, r0:r0 + QBLK, :] = qp_ref[0, :, h * MLA_ROPE:(h + 1) * MLA_ROPE].astype(BF16)
    m_s[...] = jnp.full(m_s.shape, -jnp.inf, F32)
    l_s[...] = jnp.zeros(l_s.shape, F32)
    acc_s[...] = jnp.zeros(acc_s.shape, F32)
    q_pos = qi * QBLK + lax.broadcasted_iota(jnp.int32, (grows, MLA_KBLK), 0) % QBLK
    col = lax.broadcasted_iota(jnp.int32, (grows, MLA_KBLK), 1)

    def kb_body(kb, carry):
        off = pl.multiple_of(kb * MLA_KBLK, MLA_KBLK)
        c = ckv_ref[0, pl.ds(off, MLA_KBLK), :].astype(BF16)
        pe = kpe_ref[0, pl.ds(off, MLA_KBLK), 0:MLA_ROPE].astype(BF16)
        valid = off + col <= q_pos
        for g in range(G):
            s = (_dot_nt(ql_s[g], c) + _dot_nt(qp_s[g], pe)) * MLA_SCALE
            _softmax_step(jnp.where(valid, s, -jnp.inf), c, m_s.at[g], l_s.at[g], acc_s.at[g])
        return carry

    n_kb = (qi * QBLK + QBLK + MLA_KBLK - 1) // MLA_KBLK
    lax.fori_loop(0, n_kb, kb_body, 0)
    for h in range(MLA_HEADS):
        g, r0 = h // hpg, (h % hpg) * QBLK
        o_lat = acc_s[g, r0:r0 + QBLK, :] / l_s[g, r0:r0 + QBLK, :]
        o_ref[0, :, h * MLA_V:(h + 1) * MLA_V] = _dot(o_lat.astype(BF16), wuv_ref[h])


def _mla_prompt(q_lat, q_pe, ckv, kpe, w_uv):
    B, T, _ = q_lat.shape
    assert T % MLA_KBLK == 0
    G = MLA_ROW_GROUPS
    grows = MLA_HEADS // G * QBLK
    return pl.pallas_call(
        _mla_prompt_body,
        out_shape=jax.ShapeDtypeStruct((B, T, MLA_WIDTH), F32),
        grid=(B, T // QBLK),
        in_specs=[pl.BlockSpec((1, QBLK, q_lat.shape[2]), lambda b, i: (b, i, 0)),
                  pl.BlockSpec((1, QBLK, q_pe.shape[2]), lambda b, i: (b, i, 0)),
                  pl.BlockSpec((1, T, MLA_KV_RANK), lambda b, i: (b, 0, 0)),
                  pl.BlockSpec((1, T, KR_PAD), lambda b, i: (b, 0, 0)),
                  pl.BlockSpec(w_uv.shape, lambda b, i: (0, 0, 0))],
        out_specs=pl.BlockSpec((1, QBLK, MLA_WIDTH), lambda b, i: (b, i, 0)),
        scratch_shapes=[pltpu.VMEM((G, grows, MLA_KV_RANK), BF16), pltpu.VMEM((G, grows, MLA_ROPE), BF16),
                        pltpu.VMEM((G, grows, 1), F32), pltpu.VMEM((G, grows, 1), F32),
                        pltpu.VMEM((G, grows, MLA_KV_RANK), F32)],
        compiler_params=_cparams("parallel", "arbitrary"),
        name="mla_prompt",
    )(q_lat, q_pe, ckv, kpe, w_uv)


def _mla_paged_body(pt_ref, ql_ref, qp_ref, cn_ref, pn_ref, *refs, n_pg):
    c_pages, p_pages = refs[:n_pg], refs[n_pg:2 * n_pg]
    wuv_ref, o_ref, ql_s, qp_s, m_s, l_s, acc_s = refs[2 * n_pg:]
    g = pl.program_id(1)
    T = ql_ref.shape[1]
    rows = MLA_HEADS * T

    @pl.when(g == 0)
    def _():
        _stack_heads(ql_ref, qp_ref, ql_s, qp_s, T)
        m_s[...] = jnp.full(m_s.shape, -jnp.inf, F32)
        l_s[...] = jnp.zeros(l_s.shape, F32)
        acc_s[...] = jnp.zeros(acc_s.shape, F32)
        t_pos = lax.broadcasted_iota(jnp.int32, (rows, PAGE_SIZE), 0) % T
        col = lax.broadcasted_iota(jnp.int32, (rows, PAGE_SIZE), 1)
        c = cn_ref[0].astype(BF16)
        pe = pn_ref[0, :, 0:MLA_ROPE].astype(BF16)
        s = (_dot_nt(ql_s[...].astype(BF16), c) + _dot_nt(qp_s[...].astype(BF16), pe)) * MLA_SCALE
        _softmax_step(jnp.where(col <= t_pos, s, -jnp.inf), c, m_s, l_s, acc_s)

    q_lat, q_pe = ql_s[...].astype(BF16), qp_s[...].astype(BF16)
    c_all = jnp.concatenate([c_pages[p][...] for p in range(n_pg)], axis=0).astype(BF16)
    pe_all = jnp.concatenate([p_pages[p][...] for p in range(n_pg)], axis=1).astype(BF16)
    s = (_dot_nt(q_lat, c_all) + _dot(q_pe, pe_all)) * MLA_SCALE
    m_old = m_s[...]
    m_new = jnp.maximum(m_old, jnp.max(s, axis=-1, keepdims=True))
    alpha = jnp.exp(m_old - m_new)
    e = jnp.exp(s - m_new)
    l = alpha * l_s[...] + jnp.sum(e, axis=-1, keepdims=True)
    acc = alpha * acc_s[...] + _dot(e.astype(BF16), c_all)
    m_s[...] = m_new
    l_s[...] = l
    acc_s[...] = acc

    @pl.when(g == pl.num_programs(1) - 1)
    def _():
        _mla_finish(o_ref, wuv_ref, l_s, acc_s, T)


def _mla_paged(q_lat, q_pe, ckv_new, kpe_new, cache_ckv, cache_kpe_t, page_table, j, w_uv):
    B, T, _ = q_lat.shape
    n_pages = page_table.shape[1]
    n_pg = PAGES_PER_STEP
    assert n_pages % n_pg == 0
    rows = MLA_HEADS * T

    def page_spec(shape, p):
        return pl.BlockSpec((None, None) + shape, lambda b, g, pt: (pt[b, g * n_pg + p], j, 0, 0))

    def seq_spec(t, width):
        return pl.BlockSpec((1, t, width), lambda b, g, pt: (b, 0, 0))

    return pl.pallas_call(
        functools.partial(_mla_paged_body, n_pg=n_pg),
        out_shape=jax.ShapeDtypeStruct((B, T, MLA_WIDTH), F32),
        grid_spec=pltpu.PrefetchScalarGridSpec(
            num_scalar_prefetch=1, grid=(B, n_pages // n_pg),
            in_specs=[seq_spec(T, q_lat.shape[2]), seq_spec(T, q_pe.shape[2]),
                      seq_spec(PAGE_SIZE, MLA_KV_RANK), seq_spec(PAGE_SIZE, KR_PAD)]
            + [page_spec((PAGE_SIZE, MLA_KV_RANK), p) for p in range(n_pg)]
            + [page_spec((MLA_ROPE, PAGE_SIZE), p) for p in range(n_pg)]
            + [pl.BlockSpec(w_uv.shape, lambda b, g, pt: (0, 0, 0))],
            out_specs=seq_spec(T, MLA_WIDTH),
            scratch_shapes=[pltpu.VMEM((rows, MLA_KV_RANK), F32), pltpu.VMEM((rows, MLA_ROPE), F32),
                            pltpu.VMEM((rows, 1), F32), pltpu.VMEM((rows, 1), F32),
                            pltpu.VMEM((rows, MLA_KV_RANK), F32)]),
        compiler_params=_cparams("parallel", "arbitrary"),
        name="mla_paged",
    )(page_table, q_lat, q_pe, ckv_new, kpe_new, *([cache_ckv] * n_pg), *([cache_kpe_t] * n_pg), w_uv)


S5_SEQS = 8


def _s5_body(u_ref, re0_ref, im0_ref, are_ref, aim_ref, bre_ref, bim_ref, cre_ref, cim_ref, d_ref, wglu_ref,
             bglu_ref, o_ref, re_out_ref, im_out_ref, xre_s, xim_s, re_s, im_s, *, tc):
    tj = pl.program_id(1)
    rows = tc * S5_SEQS

    @pl.when(tj == 0)
    def _():
        re_s[...] = re0_ref[...]
        im_s[...] = im0_ref[...]

    u = u_ref[...].reshape(rows, S5_WIDTH)
    ub = u.astype(BF16)
    xre_s[...] = _dot(ub, bre_ref[...]).reshape(tc, S5_SEQS, S5_FLAT)
    xim_s[...] = _dot(ub, bim_ref[...]).reshape(tc, S5_SEQS, S5_FLAT)
    a_re, a_im = are_ref[...], aim_ref[...]

    def step(t, carry):
        re, im = carry
        n_re = a_re * re - a_im * im + xre_s[t]
        n_im = a_re * im + a_im * re + xim_s[t]
        xre_s[t] = n_re
        xim_s[t] = n_im
        return n_re, n_im

    re, im = lax.fori_loop(0, tc, step, (re_s[...], im_s[...]))
    re_s[...] = re
    im_s[...] = im
    x_re = xre_s[...].reshape(rows, S5_FLAT).astype(BF16)
    x_im = xim_s[...].reshape(rows, S5_FLAT).astype(BF16)
    y = _dot(x_re, cre_ref[...]) - _dot(x_im, cim_ref[...]) + d_ref[...] * u
    y = 0.5 * y * (1.0 + lax.erf(y * (1.0 / math.sqrt(2.0))))
    out = y * jax.nn.sigmoid(_dot(y.astype(BF16), wglu_ref[...]) + bglu_ref[...])
    o_ref[...] = out.reshape(tc, S5_SEQS, S5_WIDTH)

    @pl.when(tj == pl.num_programs(1) - 1)
    def _():
        re_out_ref[...] = re
        im_out_ref[...] = im


def _s5(u_t, re0, im0, prm, tc):
    T, S, _ = u_t.shape
    assert S % S5_SEQS == 0 and T % tc == 0
    names = ("a_re", "a_im", "b_re", "b_im", "c_re", "c_im", "d", "w_glu", "b_glu")
    state_spec = pl.BlockSpec((S5_SEQS, S5_FLAT), lambda i, j: (i, 0))
    row_spec = pl.BlockSpec((tc, S5_SEQS, S5_WIDTH), lambda i, j: (j, i, 0))
    return pl.pallas_call(
        functools.partial(_s5_body, tc=tc),
        out_shape=[jax.ShapeDtypeStruct((T, S, S5_WIDTH), F32), jax.ShapeDtypeStruct((S, S5_FLAT), F32),
                   jax.ShapeDtypeStruct((S, S5_FLAT), F32)],
        grid=(S // S5_SEQS, T // tc),
        in_specs=[row_spec, state_spec, state_spec]
        + [pl.BlockSpec(prm[n].shape, lambda i, j: (0, 0)) for n in names],
        out_specs=[row_spec, state_spec, state_spec],
        scratch_shapes=[pltpu.VMEM((tc, S5_SEQS, S5_FLAT), F32)] * 2 + [pltpu.VMEM((S5_SEQS, S5_FLAT), F32)] * 2,
        compiler_params=_cparams("parallel", "arbitrary"),
        name="s5",
    )(u_t, re0, im0, *(prm[n] for n in names))


def _router_body(x_ref, g_ref, sc_ref, sh_ref, wr_ref, br_ref, h_ref, comb_ref, *, sb, tb):
    rows = sb * tb
    x = x_ref[...]
    h = x * lax.rsqrt(jnp.mean(x * x, axis=-1, keepdims=True) + NORM_EPS) * g_ref[...]
    h = h * (1.0 + sc_ref[...]) + sh_ref[...]
    h_ref[...] = h
    logits = jnp.dot(h.reshape(rows, D_MODEL), wr_ref[...], precision=HIGHEST,
                     preferred_element_type=F32) + br_ref[...]
    lane = lax.broadcasted_iota(jnp.int32, (rows, N_EXPERTS), 1)
    work = logits
    vals, sels = [], []
    for _ in range(TOP_K):
        m = jnp.max(work, axis=-1, keepdims=True)
        idx = jnp.min(jnp.where(work == m, lane, N_EXPERTS), axis=-1, keepdims=True)
        sel = lane == idx
        vals.append(m)
        sels.append(sel)
        work = jnp.where(sel, -jnp.inf, work)
    exps = [jnp.exp(v - vals[0]) for v in vals]
    denom = exps[0] + exps[1] + exps[2] + exps[3]
    comb = jnp.zeros((rows, N_EXPERTS), F32)
    for e, sel in zip(exps, sels):
        comb = jnp.where(sel, e / denom, comb)
    comb_ref[...] = comb.reshape(sb, tb, N_EXPERTS)


def _router(x, g, scale, shift, w_r, b_r, rows=512):
    S, T, D = x.shape
    sb, tb = _blocking(S, T, rows)
    row = pl.BlockSpec((sb, tb, D), lambda i, j: (i, j, 0))
    seq = pl.BlockSpec((sb, 1, D), lambda i, j: (i, 0, 0))
    return pl.pallas_call(
        functools.partial(_router_body, sb=sb, tb=tb),
        out_shape=[jax.ShapeDtypeStruct((S, T, D), F32), jax.ShapeDtypeStruct((S, T, N_EXPERTS), F32)],
        grid=(S // sb, T // tb),
        in_specs=[row, pl.BlockSpec((1, D), lambda i, j: (0, 0)), seq, seq,
                  pl.BlockSpec((D, N_EXPERTS), lambda i, j: (0, 0)),
                  pl.BlockSpec((1, N_EXPERTS), lambda i, j: (0, 0))],
        out_specs=[row, pl.BlockSpec((sb, tb, N_EXPERTS), lambda i, j: (i, j, 0))],
        compiler_params=_cparams("parallel", "parallel"),
        name="router",
    )(x, g.reshape(1, D), scale, shift, w_r, b_r.reshape(1, N_EXPERTS))


def _experts_body(h_ref, comb_ref, x_ref, gate_ref, wg_ref, bg_ref, wl_ref, bl_ref, wd_ref, bd_ref, o_ref,
                  hb_s, acc_s, *, sb, tb):
    e = pl.program_id(2)
    rows = sb * tb

    @pl.when(e == 0)
    def _():
        hb_s[...] = h_ref[...].reshape(rows, D_MODEL).astype(BF16)
        acc_s[...] = jnp.zeros(acc_s.shape, F32)

    hb = hb_s[...]
    glu = jnp.minimum(_dot(hb, wg_ref[0]) + bg_ref[0], SWIGLU_LIMIT)
    lin = jnp.clip(_dot(hb, wl_ref[0]) + bl_ref[0], -SWIGLU_LIMIT, SWIGLU_LIMIT)
    act = glu * jax.nn.sigmoid(SWIGLU_ALPHA * glu) * (lin + 1.0)
    y = _dot(act.astype(BF16), wd_ref[0]) + bd_ref[0]
    comb = comb_ref[...].reshape(rows, N_EXPERTS)
    lane = lax.broadcasted_iota(jnp.int32, (rows, N_EXPERTS), 1)
    w_e = jnp.sum(jnp.where(lane == e, comb, 0.0), axis=-1, keepdims=True)
    acc_s[...] = acc_s[...] + w_e * y

    @pl.when(e == pl.num_programs(2) - 1)
    def _():
        o_ref[...] = x_ref[...] + gate_ref[...] * acc_s[...].reshape(sb, tb, D_MODEL)


def _experts(h, comb, x, gate, prm, rows=1024):
    S, T, D = x.shape
    sb, tb = _blocking(S, T, rows)
    F = D_FF_EXPERT
    row = pl.BlockSpec((sb, tb, D), lambda i, j, e: (i, j, 0))
    return pl.pallas_call(
        functools.partial(_experts_body, sb=sb, tb=tb),
        out_shape=jax.ShapeDtypeStruct((S, T, D), F32),
        grid=(S // sb, T // tb, N_EXPERTS),
        in_specs=[row, pl.BlockSpec((sb, tb, N_EXPERTS), lambda i, j, e: (i, j, 0)), row,
                  pl.BlockSpec((sb, 1, D), lambda i, j, e: (i, 0, 0)),
                  pl.BlockSpec((1, D, F), lambda i, j, e: (e, 0, 0)),
                  pl.BlockSpec((1, 1, F), lambda i, j, e: (e, 0, 0)),
                  pl.BlockSpec((1, D, F), lambda i, j, e: (e, 0, 0)),
                  pl.BlockSpec((1, 1, F), lambda i, j, e: (e, 0, 0)),
                  pl.BlockSpec((1, F, D), lambda i, j, e: (e, 0, 0)),
                  pl.BlockSpec((1, 1, D), lambda i, j, e: (e, 0, 0))],
        out_specs=row,
        scratch_shapes=[pltpu.VMEM((sb * tb, D), BF16), pltpu.VMEM((sb * tb, D), F32)],
        compiler_params=_cparams("parallel", "parallel", "arbitrary"),
        name="experts",
    )(h, comb, x, gate, prm["w_g"], prm["b_g"], prm["w_l"], prm["b_l"], prm["w_d"], prm["b_d"])


def _rw_column_order():
    W, dl, al = RW_WIDTH, RW_DECAY_LORA, RW_AAA_LORA
    r = np.arange(0, W)
    wd = np.arange(W, W + dl)
    k = np.arange(W + dl, 2 * W + dl)
    v = np.arange(2 * W + dl, 3 * W + dl)
    rest = np.arange(3 * W + dl, RW_PROJ)
    return np.concatenate([r, k, v, wd, rest])


def _q_up_column_order():
    per = MLA_NOPE + MLA_ROPE
    nope = np.concatenate([np.arange(h * per, h * per + MLA_NOPE) for h in range(MLA_HEADS)])
    rope = np.concatenate([np.arange(h * per + MLA_NOPE, (h + 1) * per) for h in range(MLA_HEADS)])
    return np.concatenate([nope, rope])


def _block_diag(blocks):
    G, a, b = blocks.shape
    eye = jnp.eye(G, dtype=blocks.dtype)
    return (eye[:, None, :, None] * blocks[:, :, None, :]).reshape(G * a, G * b)


def _even_params(P, j):
    order = _rw_column_order()
    w_in = P["ev_w_in"][j]
    row = lambda a: a.reshape(1, -1).astype(F32)
    return dict(
        w_in=jnp.concatenate([w_in[:, :RW_PROJ][:, order], w_in[:, RW_PROJ:]], axis=1).astype(BF16),
        w_out_rw=P["ev_w_out"][j][:RW_WIDTH].astype(BF16),
        w_out_sb=P["ev_w_out"][j][RW_WIDTH:].astype(BF16),
        rw=dict(mu=row(P["rw_mu"][j][order]), w0=row(P["rw_w0"][j]), w2=P["rw_w2"][j].astype(BF16),
                a0=row(P["rw_a0"][j]), a2=P["rw_a2"][j].astype(BF16), g2=P["rw_g2"][j].astype(BF16),
                k_k=row(P["rw_k_k"][j]), k_a=row(P["rw_k_a"][j]), r_k=row(P["rw_r_k"][j]),
                ln_w=row(P["rw_ln_w"][j]), ln_b=row(P["rw_ln_b"][j])))


def _odd_params(P, j):
    row = lambda a: a.reshape(1, -1).astype(F32)
    w_in = P["od_w_in"][j]
    pad = jnp.zeros((D_MODEL, KR_PAD - MLA_ROPE), w_in.dtype)
    lr, li = P["s5_lam_re"][j].astype(F32), P["s5_lam_im"][j].astype(F32)
    dt = jnp.exp(P["s5_log_step"][j].astype(F32))[:, None]
    mag = jnp.exp(lr * dt)
    ab_re, ab_im = mag * jnp.cos(li * dt), mag * jnp.sin(li * dt)
    den = lr * lr + li * li
    f_re = ((ab_re - 1.0) * lr + ab_im * li) / den
    f_im = (ab_im * lr - (ab_re - 1.0) * li) / den
    b_re, b_im = P["s5_b_re"][j].astype(F32), P["s5_b_im"][j].astype(F32)
    bb_re = f_re[..., None] * b_re - f_im[..., None] * b_im
    bb_im = f_re[..., None] * b_im + f_im[..., None] * b_re
    to_in = lambda a: _block_diag(jnp.swapaxes(a, 1, 2)).astype(BF16)
    to_out = lambda a: _block_diag(jnp.swapaxes(a, 1, 2)).astype(BF16)
    return dict(
        w_in=jnp.concatenate([w_in, pad], axis=1).astype(BF16),
        w_out_s5=P["od_w_out"][j][:S5_WIDTH].astype(BF16),
        w_out_mla=P["od_w_out"][j][S5_WIDTH:].astype(BF16),
        s5=dict(a_re=row(ab_re), a_im=row(ab_im), b_re=to_in(bb_re), b_im=to_in(bb_im),
                c_re=to_out(P["s5_c_re"][j].astype(F32)), c_im=to_out(P["s5_c_im"][j].astype(F32)),
                d=row(P["s5_d"][j]), w_glu=P["s5_w_glu"][j].astype(BF16), b_glu=row(P["s5_b_glu"][j])),
        mla=dict(q_g=row(P["mla_q_norm_g"][j]), w_q=P["mla_w_q_up"][j][:, _q_up_column_order()].astype(BF16),
                 kv_g=row(P["mla_kv_norm_g"][j]),
                 w_uk=jnp.transpose(P["mla_w_uk"][j], (1, 2, 0)).astype(BF16),
                 w_uv=jnp.transpose(P["mla_w_uv"][j], (1, 0, 2)).astype(BF16)))


def _deinterleave_body(w_ref, sel_ref, even_ref, odd_ref):
    half = even_ref.shape[-1]
    y = _dot(w_ref[0].astype(BF16), sel_ref[...])
    even_ref[0] = y[:, :half].astype(BF16)
    odd_ref[0] = y[:, half:].astype(BF16)


def _deinterleave(w):
    E, D, F2 = w.shape
    F = F2 // 2
    col = np.arange(F2)
    sel = (np.arange(F2)[:, None] == (2 * (col % F) + col // F)[None, :]).astype(np.float32)
    return pl.pallas_call(
        _deinterleave_body,
        out_shape=[jax.ShapeDtypeStruct((E, D, F), BF16)] * 2,
        grid=(E,),
        in_specs=[pl.BlockSpec((1, D, F2), lambda e: (e, 0, 0)), pl.BlockSpec((F2, F2), lambda e: (0, 0))],
        out_specs=[pl.BlockSpec((1, D, F), lambda e: (e, 0, 0))] * 2,
        compiler_params=_cparams("parallel"),
        name="deinterleave",
    )(w, jnp.asarray(sel, BF16))


def _moe_params(P, l):
    w_gu, b_gu = P["moe_w_gu"][l], P["moe_b_gu"][l]
    w_g, w_l = _deinterleave(w_gu)
    return dict(w_r=P["moe_w_router"][l].astype(F32), b_r=P["moe_b_router"][l].astype(F32),
                w_g=w_g, w_l=w_l,
                b_g=b_gu[:, None, 0::2].astype(F32), b_l=b_gu[:, None, 1::2].astype(F32),
                w_d=P["moe_w_down"][l].astype(BF16), b_d=P["moe_b_down"][l][:, None, :].astype(F32))


def _rope_tables(q_pos):
    inv = jnp.power(ROPE_THETA, -jnp.arange(ROPE_HALF, dtype=F32) / ROPE_HALF)
    ang = q_pos.astype(F32)[:, None] * inv[None, :]
    cos, sin = jnp.cos(ang), jnp.sin(ang)
    q_rep = MLA_HEADS * MLA_ROPE // ROPE_HALF
    k_rep = KR_PAD // ROPE_HALF
    return jnp.tile(cos, (1, q_rep)), jnp.tile(sin, (1, q_rep)), jnp.tile(cos, (1, k_rep)), jnp.tile(sin, (1, k_rep))


def _pad_rows(a, n):
    return jnp.pad(a, ((0, 0), (0, n - a.shape[1]), (0, 0)))


def _trunk(x, c, q_pos, shift0, wkv0, s5re0, s5im0, paged, P, prepared):
    S, T, D = x.shape
    new = {name: [] for name in ("sb_k", "sb_v", "ckv", "kpe", "wkv", "shift", "s5re", "s5im")}
    tables = _rope_tables(q_pos)
    if paged is not None:
        paged = dict(paged, sb_k=jnp.transpose(paged["sb_k"], (0, 1, 3, 4, 2)),
                     sb_v=jnp.transpose(paged["sb_v"], (0, 1, 3, 4, 2)),
                     kpe=jnp.transpose(paged["kpe"], (0, 1, 3, 2)))
    rw_sb, rw_tb = (S5_SEQS, T) if T < 256 else (4, 256)
    for l in range(DEPTH):
        j = l // 2
        mod = _linear(c[None], P["w_mod"][l].astype(BF16), b=P["b_mod"][l], silu=True, tn=1536)[0]
        sh1, sc1, g1, sh2, sc2, g2 = (mod[0, :, i * D:(i + 1) * D][:, None, :] for i in range(6))
        if l % 2 == 0:
            ep = prepared["even"][j]
            rw, q, k, v = _linear(x, ep["w_in"], norm_g=P["norm_mix_g"][l], scale=sc1, shift=sh1,
                                  splits=[RW_PROJ, SB_WIDTH, SB_WIDTH, SB_WIDTH], rows=256)
            shift = _normmod(x[:, T - 1:T], P["norm_mix_g"][l], sc1, sh1)
            if shift0 is None:
                prev = jnp.zeros((S, 1, RW_PROJ), F32)
            else:
                prev = _linear(shift0[:, j][None], ep["w_in"][:, :RW_PROJ])[0].reshape(S, 1, RW_PROJ)
            rw_out, wkv = _rwkv(rw, prev, None if wkv0 is None else wkv0[:, j], ep["rw"], sb=rw_sb, tb=rw_tb)
            if paged is None:
                sb_out = _sb_prompt(q, k, v)
            else:
                sb_out = _sb_paged(q, _pad_rows(k, PAGE_SIZE), _pad_rows(v, PAGE_SIZE), paged["sb_k"],
                                   paged["sb_v"], paged["page_table"], j)
            x = _linear(rw_out, ep["w_out_rw"], x2=sb_out, w2=ep["w_out_sb"], res=x, gate=g1)[0]
            new["shift"].append(shift[:, 0])
            new["wkv"].append(wkv)
            new["sb_k"].append(k.reshape(S, T, SB_HEADS, SB_HEAD_DIM))
            new["sb_v"].append(v.reshape(S, T, SB_HEADS, SB_HEAD_DIM))
        else:
            op = prepared["odd"][j]
            u, qd, kvd, kr = _linear(x, op["w_in"], norm_g=P["norm_mix_g"][l], scale=sc1, shift=sh1,
                                     splits=[S5_WIDTH, MLA_Q_RANK, MLA_KV_RANK, KR_PAD], rows=256)
            re0 = jnp.zeros((S, S5_FLAT), F32) if s5re0 is None else s5re0[:, j].reshape(S, S5_FLAT)
            im0 = jnp.zeros((S, S5_FLAT), F32) if s5im0 is None else s5im0[:, j].reshape(S, S5_FLAT)
            s5_t, s_re, s_im = _s5(jnp.swapaxes(u, 0, 1), re0, im0, op["s5"], tc=min(T, 128))
            s5_out = jnp.swapaxes(s5_t, 0, 1)
            q_lat, q_pe, ckv, kpe = _mla_prep(qd, kvd, kr, tables, op["mla"])
            if paged is None:
                mla_out = _mla_prompt(q_lat, q_pe, ckv, kpe, op["mla"]["w_uv"])
            else:
                mla_out = _mla_paged(q_lat, q_pe, _pad_rows(ckv, PAGE_SIZE), _pad_rows(kpe, PAGE_SIZE),
                                     paged["ckv"], paged["kpe"], paged["page_table"], j, op["mla"]["w_uv"])
            x = _linear(s5_out, op["w_out_s5"], x2=mla_out, w2=op["w_out_mla"], res=x, gate=g1)[0]
            new["s5re"].append(s_re.reshape(S, S5_GROUPS, S5_STATE))
            new["s5im"].append(s_im.reshape(S, S5_GROUPS, S5_STATE))
            new["ckv"].append(ckv)
            new["kpe"].append(kpe[..., :MLA_ROPE])
        mp = prepared["moe"][l]
        h, comb = _router(x, P["norm_ffn_g"][l], sc2, sh2, mp["w_r"], mp["b_r"])
        x = _experts(h, comb, x, g2, mp)
    y = _normmod(x, P["norm_final_g"])
    return y, {name: jnp.stack(vals, axis=1) for name, vals in new.items()}


def kernel(x_prompt, x_sample, cache_sb_k, cache_sb_v, cache_mla_ckv, cache_mla_kpe, state_rwkv_wkv,
           state_rwkv_shift, state_s5_re, state_s5_im, page_table, c_prompt, c_sample, w_mod, b_mod,
           norm_mix_g, norm_ffn_g, norm_final_g, ev_w_in, ev_w_out, rw_mu, rw_w0, rw_w2, rw_a0, rw_a2, rw_g2,
           rw_k_k, rw_k_a, rw_r_k, rw_ln_w, rw_ln_b, od_w_in, od_w_out, s5_lam_re, s5_lam_im, s5_b_re, s5_b_im,
           s5_c_re, s5_c_im, s5_d, s5_log_step, s5_w_glu, s5_b_glu, mla_q_norm_g, mla_w_q_up, mla_kv_norm_g,
           mla_w_uk, mla_w_uv, moe_w_router, moe_b_router, moe_w_gu, moe_b_gu, moe_w_down, moe_b_down):
    P = dict(w_mod=w_mod, b_mod=b_mod, norm_mix_g=norm_mix_g, norm_ffn_g=norm_ffn_g, norm_final_g=norm_final_g,
             ev_w_in=ev_w_in, ev_w_out=ev_w_out, rw_mu=rw_mu, rw_w0=rw_w0, rw_w2=rw_w2, rw_a0=rw_a0, rw_a2=rw_a2,
             rw_g2=rw_g2, rw_k_k=rw_k_k, rw_k_a=rw_k_a, rw_r_k=rw_r_k, rw_ln_w=rw_ln_w, rw_ln_b=rw_ln_b,
             od_w_in=od_w_in, od_w_out=od_w_out, s5_lam_re=s5_lam_re, s5_lam_im=s5_lam_im, s5_b_re=s5_b_re,
             s5_b_im=s5_b_im, s5_c_re=s5_c_re, s5_c_im=s5_c_im, s5_d=s5_d, s5_log_step=s5_log_step,
             s5_w_glu=s5_w_glu, s5_b_glu=s5_b_glu, mla_q_norm_g=mla_q_norm_g, mla_w_q_up=mla_w_q_up,
             mla_kv_norm_g=mla_kv_norm_g, mla_w_uk=mla_w_uk, mla_w_uv=mla_w_uv, moe_w_router=moe_w_router,
             moe_b_router=moe_b_router, moe_w_gu=moe_w_gu, moe_b_gu=moe_b_gu, moe_w_down=moe_w_down,
             moe_b_down=moe_b_down)
    n_even, n_odd = (DEPTH + 1) // 2, DEPTH // 2
    prepared = dict(even=[_even_params(P, j) for j in range(n_even)],
                    odd=[_odd_params(P, j) for j in range(n_odd)],
                    moe=[_moe_params(P, l) for l in range(DEPTH)])
    sp = x_prompt.shape[1]
    y_prompt, pn = _trunk(x_prompt, c_prompt, jnp.arange(sp, dtype=jnp.int32), None, None, None, None, None,
                          P, prepared)
    past_len = page_table.shape[1] * cache_sb_k.shape[2]
    paged = dict(page_table=page_table, sb_k=cache_sb_k, sb_v=cache_sb_v, ckv=cache_mla_ckv, kpe=cache_mla_kpe)
    y_sample, sn = _trunk(x_sample, c_sample, past_len + jnp.arange(x_sample.shape[1], dtype=jnp.int32),
                          state_rwkv_shift, state_rwkv_wkv, state_s5_re, state_s5_im, paged, P, prepared)
    return (y_prompt, y_sample,
            pn["sb_k"], pn["sb_v"], pn["ckv"], pn["kpe"], pn["wkv"], pn["shift"], pn["s5re"], pn["s5im"],
            sn["sb_k"], sn["sb_v"], sn["ckv"], sn["kpe"], sn["wkv"], sn["shift"], sn["s5re"], sn["s5im"])
```

```python
import functools
import math

import numpy as np
import jax
import jax.numpy as jnp
from jax import lax
from jax.experimental import pallas as pl
from jax.experimental.pallas import tpu as pltpu

F32 = jnp.float32
BF16 = jnp.bfloat16
HIGHEST = lax.Precision.HIGHEST

D_MODEL = 1024
DEPTH = 4
PAGE_SIZE = 128
NORM_EPS = 1e-6

RW_HEADS = 8
RW_HEAD_DIM = 64
RW_WIDTH = RW_HEADS * RW_HEAD_DIM
RW_DECAY_LORA = 64
RW_AAA_LORA = 64
RW_GATE_LORA = 128
RW_PROJ = 3 * RW_WIDTH + RW_DECAY_LORA + RW_AAA_LORA + RW_GATE_LORA
RW_GN_EPS = 64e-5

SB_HEADS = 8
SB_HEAD_DIM = 64
SB_WIDTH = SB_HEADS * SB_HEAD_DIM
SB_SCALE = SB_HEAD_DIM ** -0.5

S5_GROUPS = 32
S5_GROUP_CH = 16
S5_WIDTH = S5_GROUPS * S5_GROUP_CH
S5_STATE = 64
S5_FLAT = S5_GROUPS * S5_STATE

MLA_HEADS = 8
MLA_Q_RANK = 384
MLA_KV_RANK = 256
MLA_NOPE = 64
MLA_ROPE = 32
MLA_V = 64
MLA_WIDTH = MLA_HEADS * MLA_V
MLA_SCALE = (MLA_NOPE + MLA_ROPE) ** -0.5
ROPE_THETA = 10000.0
ROPE_HALF = MLA_ROPE // 2
KR_PAD = 128

N_EXPERTS = 32
TOP_K = 4
D_FF_EXPERT = 512
SWIGLU_LIMIT = 7.0
SWIGLU_ALPHA = 1.702

VMEM_LIMIT_BYTES = 56 * 1024 * 1024
QBLK = 128
SB_QBLK = 512
SB_KBLK = 256
SB_HEAD_GROUP = 4
MLA_KBLK = 512
PAGES_PER_STEP = 16
EXPERTS_PER_STEP = 2


def _cparams(*sem):
    return pltpu.CompilerParams(dimension_semantics=sem, vmem_limit_bytes=VMEM_LIMIT_BYTES)


def _blocking(S, T, rows):
    if T >= rows:
        assert T % rows == 0
        return 1, rows
    sb = max(1, min(S, rows // T))
    assert S % sb == 0
    return sb, T


def _softplus(x):
    return jnp.maximum(x, 0.0) + jnp.log1p(jnp.exp(-jnp.abs(x)))


def _dot(a, b):
    return jnp.dot(a, b, preferred_element_type=F32)


def _dot_nt(a, b):
    return lax.dot_general(a, b, (((1,), (1,)), ((), ())), preferred_element_type=F32)


def _split_hi_lo(x):
    hi = x.astype(BF16)
    lo = (x - hi.astype(F32)).astype(BF16)
    return hi, lo


def _linear_body(*refs, n_out, splits, has_norm, has_mod, silu, has_x2, has_b, has_res, sb, tb):
    outs = refs[len(refs) - n_out:]
    it = iter(refs[:len(refs) - n_out])
    x = next(it)[...]
    if has_norm:
        g_ref = next(it)
        x = x * lax.rsqrt(jnp.mean(x * x, axis=-1, keepdims=True) + NORM_EPS) * g_ref[...]
    if has_mod:
        sc_ref, sh_ref = next(it), next(it)
        x = x * (1.0 + sc_ref[...]) + sh_ref[...]
    if silu:
        x = x * jax.nn.sigmoid(x)
    w_ref = next(it)
    rows = sb * tb
    y = _dot(x.reshape(rows, x.shape[-1]).astype(BF16), w_ref[...])
    if has_x2:
        x2_ref, w2_ref = next(it), next(it)
        x2 = x2_ref[...]
        y = y + _dot(x2.reshape(rows, x2.shape[-1]).astype(BF16), w2_ref[...])
    if has_b:
        y = y + next(it)[...]
    if has_res:
        res_ref, gate_ref = next(it), next(it)
    off = 0
    for o_ref, n in zip(outs, splits):
        piece = y[:, off:off + n].reshape(sb, tb, n)
        if has_res:
            piece = res_ref[...] + gate_ref[...] * piece
        o_ref[...] = piece
        off += n


def _linear(x, w, *, b=None, norm_g=None, scale=None, shift=None, silu=False, x2=None, w2=None,
            res=None, gate=None, splits=None, rows=512, tn=None):
    S, T, K = x.shape
    N = w.shape[1]
    splits = list(splits) if splits is not None else [N]
    tn = N if tn is None else tn
    assert N % tn == 0 and (tn == N or len(splits) == 1)
    assert res is None or len(splits) == 1
    sb, tb = _blocking(S, T, rows)
    grid = (S // sb, T // tb, N // tn)

    def row_spec(width):
        return pl.BlockSpec((sb, tb, width), lambda i, j, n: (i, j, 0))

    def seq_spec(width):
        return pl.BlockSpec((sb, 1, width), lambda i, j, n: (i, 0, 0))

    args, specs = [x], [row_spec(K)]
    if norm_g is not None:
        args.append(norm_g.reshape(1, K))
        specs.append(pl.BlockSpec((1, K), lambda i, j, n: (0, 0)))
    if scale is not None:
        args += [scale, shift]
        specs += [seq_spec(K), seq_spec(K)]
    args.append(w)
    specs.append(pl.BlockSpec((K, tn), lambda i, j, n: (0, n)))
    if x2 is not None:
        K2 = x2.shape[-1]
        args += [x2, w2]
        specs += [row_spec(K2), pl.BlockSpec((K2, tn), lambda i, j, n: (0, n))]
    if b is not None:
        args.append(b.reshape(1, N))
        specs.append(pl.BlockSpec((1, tn), lambda i, j, n: (0, n)))
    if res is not None:
        args += [res, gate]
        specs += [pl.BlockSpec((sb, tb, tn), lambda i, j, n: (i, j, n)),
                  pl.BlockSpec((sb, 1, tn), lambda i, j, n: (i, 0, n))]
    if tn == N:
        out_specs = [row_spec(n) for n in splits]
        blk_splits = splits
    else:
        out_specs = [pl.BlockSpec((sb, tb, tn), lambda i, j, n: (i, j, n))]
        blk_splits = [tn]
    body = functools.partial(
        _linear_body, n_out=len(splits), splits=blk_splits, has_norm=norm_g is not None,
        has_mod=scale is not None, silu=silu, has_x2=x2 is not None, has_b=b is not None,
        has_res=res is not None, sb=sb, tb=tb)
    return pl.pallas_call(
        body,
        out_shape=[jax.ShapeDtypeStruct((S, T, n), F32) for n in splits],
        grid=grid, in_specs=specs, out_specs=out_specs,
        compiler_params=_cparams("parallel", "parallel", "arbitrary"),
        name="linear",
    )(*args)


def _normmod_body(*refs, has_mod):
    it = iter(refs)
    x = next(it)[...]
    g_ref = next(it)
    y = x * lax.rsqrt(jnp.mean(x * x, axis=-1, keepdims=True) + NORM_EPS) * g_ref[...]
    if has_mod:
        sc_ref, sh_ref = next(it), next(it)
        y = y * (1.0 + sc_ref[...]) + sh_ref[...]
    next(it)[...] = y


def _normmod(x, g, scale=None, shift=None, rows=512):
    S, T, K = x.shape
    sb, tb = _blocking(S, T, rows)
    args = [x, g.reshape(1, K)]
    specs = [pl.BlockSpec((sb, tb, K), lambda i, j: (i, j, 0)), pl.BlockSpec((1, K), lambda i, j: (0, 0))]
    if scale is not None:
        args += [scale, shift]
        specs += [pl.BlockSpec((sb, 1, K), lambda i, j: (i, 0, 0))] * 2
    return pl.pallas_call(
        functools.partial(_normmod_body, has_mod=scale is not None),
        out_shape=jax.ShapeDtypeStruct((S, T, K), F32),
        grid=(S // sb, T // tb), in_specs=specs,
        out_specs=pl.BlockSpec((sb, tb, K), lambda i, j: (i, j, 0)),
        compiler_params=_cparams("parallel", "parallel"),
        name="normmod",
    )(*args)


RW_COLS = RW_PROJ


def _rwkv_body(*refs, has_state, sb, tb, chunk):
    it = iter(refs)
    rw_ref, prev_ref = next(it), next(it)
    wkv0_ref = next(it) if has_state else None
    (mu_ref, w0_ref, w2_ref, a0_ref, a2_ref, g2_ref, kk_w_ref, ka_ref, rk_ref, lnw_ref,
     lnb_ref) = (next(it) for _ in range(11))
    out_ref, wkv_ref = next(it), next(it)
    carry_s, state_s, r_s, w_s, k_s, v_s, kk_s, b_s, y_s = (next(it) for _ in range(9))
    H, N, W = RW_HEADS, RW_HEAD_DIM, RW_WIDTH
    HALF = W // 2
    rows = sb * tb
    tj = pl.program_id(1)

    @pl.when(tj == 0)
    def _():
        carry_s[...] = prev_ref[...]
        if has_state:
            for h in range(H):
                state_s[:, :, h * N:(h + 1) * N] = wkv0_ref[:, h]
        else:
            state_s[...] = jnp.zeros(state_s.shape, F32)

    x = rw_ref[...]
    rolled = pltpu.roll(x.reshape(rows, RW_COLS), 1, 0).reshape(sb, tb, RW_COLS)
    tpos = lax.broadcasted_iota(jnp.int32, (sb, tb, 1), 1)
    shifted = jnp.where(tpos == 0, carry_s[...], rolled)
    carry_s[...] = x[:, tb - 1:tb, :]
    xm = (x + (shifted - x) * mu_ref[...]).reshape(rows, RW_COLS)
    r, k, v = xm[:, 0:W], xm[:, W:2 * W], xm[:, 2 * W:3 * W]
    o = 3 * W
    wd = xm[:, o:o + RW_DECAY_LORA]
    ad = xm[:, o + RW_DECAY_LORA:o + RW_DECAY_LORA + RW_AAA_LORA]
    gd = xm[:, o + RW_DECAY_LORA + RW_AAA_LORA:RW_COLS]

    w_log = -_softplus(-(w0_ref[...] + _dot(jnp.tanh(wd).astype(BF16), w2_ref[...]))) - 0.5
    decay = jnp.exp(-jnp.exp(w_log))
    a = jax.nn.sigmoid(a0_ref[...] + _dot(ad.astype(BF16), a2_ref[...]))
    g = _dot(jax.nn.sigmoid(gd).astype(BF16), g2_ref[...])
    pr = lax.broadcasted_iota(jnp.int32, (HALF, HALF), 0) // N
    pc = lax.broadcasted_iota(jnp.int32, (HALF, HALF), 1) // N
    head_ones = (pr == pc).astype(BF16)

    def head_sums(xs, terms=2):
        m = xs[0].shape[0]
        parts = []
        for xv in xs:
            rest = xv
            for _ in range(terms):
                part = rest.astype(BF16)
                parts.append(part)
                rest = rest - part.astype(F32)
        stacked = jnp.concatenate(parts, axis=0)
        res = jnp.concatenate([_dot(stacked[:, c:c + HALF], head_ones) for c in (0, HALF)], axis=-1)
        return [sum(res[(terms * i + t) * m:(terms * i + t + 1) * m] for t in range(terms)) for i in range(len(xs))]

    kk = k * kk_w_ref[...]
    kk_sq, = head_sums([kk * kk], terms=3)
    kk = kk / jnp.maximum(jnp.sqrt(kk_sq), 1e-12)
    k = k * (1.0 + (a - 1.0) * ka_ref[...])
    for dst, src in ((r_s, r), (w_s, decay), (k_s, k), (v_s, v), (kk_s, kk), (b_s, kk * a)):
        dst[...] = src.reshape(sb, tb, W)

    lane = lax.broadcasted_iota(jnp.int32, (N, W), 1)
    sub = lax.broadcasted_iota(jnp.int32, (N, W), 0)
    diag = (lane % N == sub).astype(F32)
    m_state = sb * N

    def state_sums(xs):
        return [r_.reshape(sb, N, W) for r_ in head_sums([xv.reshape(m_state, W) for xv in xs])]

    TILE = 2 * N
    t_lane = lax.broadcasted_iota(jnp.int32, (N, TILE), 1)
    t_sub = lax.broadcasted_iota(jnp.int32, (N, TILE), 0)
    first_head = t_lane < N
    pick_first = first_head & (t_lane == t_sub)
    pick_second = (t_lane >= N) & (t_lane - N == t_sub)

    def value_columns(vt):
        tiles = []
        for c in range(0, W, TILE):
            vb = jnp.broadcast_to(vt[:, :, c:c + TILE], (sb, N, TILE))
            col_a = jnp.sum(jnp.where(pick_first, vb, 0.0), axis=-1, keepdims=True)
            col_b = jnp.sum(jnp.where(pick_second, vb, 0.0), axis=-1, keepdims=True)
            tiles.append(jnp.where(first_head, col_a, col_b))
        return jnp.concatenate(tiles, axis=-1)

    def chunk_body(c, carry):
        t0 = pl.multiple_of(c * chunk, chunk)
        y_rows = []
        st = state_s[...]
        for i in range(chunk):
            ts = pl.ds(t0 + i, 1)
            s_kk, = state_sums([st * kk_s[:, ts, :]])
            v_col = value_columns(v_s[:, ts, :])
            st = st * w_s[:, ts, :] - s_kk * b_s[:, ts, :] + v_col * k_s[:, ts, :]
            y_b, = state_sums([st * r_s[:, ts, :]])
            y_rows.append(jnp.sum(y_b * diag, axis=1, keepdims=True))
        state_s[...] = st
        y_s[:, pl.ds(t0, chunk), :] = jnp.concatenate(y_rows, axis=1)
        return carry

    lax.fori_loop(0, tb // chunk, chunk_body, 0)

    y = y_s[...].reshape(rows, W)
    y_sum, rk_sum = head_sums([y, r * k * rk_ref[...]], terms=3)
    dev = y - y_sum * (1.0 / N)
    var, = head_sums([dev * dev], terms=3)
    yn = dev * lax.rsqrt(var * (1.0 / N) + RW_GN_EPS) * lnw_ref[...] + lnb_ref[...]
    bonus = rk_sum * v
    out_ref[...] = ((yn + bonus) * g).reshape(sb, tb, W)

    @pl.when(tj == pl.num_programs(1) - 1)
    def _():
        for h in range(H):
            wkv_ref[:, h] = state_s[:, :, h * N:(h + 1) * N]


def _rwkv(rw, prev, wkv0, prm, *, sb, tb):
    S, T, _ = rw.shape
    H, N, W = RW_HEADS, RW_HEAD_DIM, RW_WIDTH
    chunk = min(tb, 16)
    has_state = wkv0 is not None
    args = [rw, prev]
    specs = [pl.BlockSpec((sb, tb, RW_COLS), lambda i, j: (i, j, 0)),
             pl.BlockSpec((sb, 1, RW_COLS), lambda i, j: (i, 0, 0))]
    if has_state:
        args.append(wkv0)
        specs.append(pl.BlockSpec((sb, H, N, N), lambda i, j: (i, 0, 0, 0)))
    for name in ("mu", "w0", "w2", "a0", "a2", "g2", "k_k", "k_a", "r_k", "ln_w", "ln_b"):
        arr = prm[name]
        args.append(arr)
        specs.append(pl.BlockSpec(arr.shape, lambda i, j: (0, 0)))
    tok_scratch = pltpu.VMEM((sb, tb, W), F32)
    return pl.pallas_call(
        functools.partial(_rwkv_body, has_state=has_state, sb=sb, tb=tb, chunk=chunk),
        out_shape=[jax.ShapeDtypeStruct((S, T, W), F32), jax.ShapeDtypeStruct((S, H, N, N), F32)],
        grid=(S // sb, T // tb), in_specs=specs,
        out_specs=[pl.BlockSpec((sb, tb, W), lambda i, j: (i, j, 0)),
                   pl.BlockSpec((sb, H, N, N), lambda i, j: (i, 0, 0, 0))],
        scratch_shapes=[pltpu.VMEM((sb, 1, RW_COLS), F32), pltpu.VMEM((sb, N, W), F32)]
        + [tok_scratch] * 7,
        compiler_params=_cparams("parallel", "arbitrary"),
        name="rwkv7",
    )(*args)


def _sb_scores(z, valid):
    log_beta = -_softplus(-z)
    log_rem = log_beta - z
    if valid is not None:
        log_rem = jnp.where(valid, log_rem, 0.0)
    return log_beta, log_rem


def _sb_weights(log_beta, log_rem, valid, tri, run):
    hi, lo = _split_hi_lo(log_rem)
    after = _dot(hi, tri) + _dot(lo, tri)
    a = jnp.exp(log_beta + after + run)
    if valid is not None:
        a = jnp.where(valid, a, 0.0)
    return a.astype(BF16), run + jnp.sum(log_rem, axis=-1, keepdims=True)


def _later_key_matrix(n):
    j = lax.broadcasted_iota(jnp.int32, (n, n), 0)
    s = lax.broadcasted_iota(jnp.int32, (n, n), 1)
    return (j > s).astype(BF16)


def _pair_later_matrix():
    n = 2 * PAGE_SIZE
    def pos(c):
        return (1 - c // PAGE_SIZE) * PAGE_SIZE + c % PAGE_SIZE
    j = pos(lax.broadcasted_iota(jnp.int32, (n, n), 0))
    s = pos(lax.broadcasted_iota(jnp.int32, (n, n), 1))
    return (j > s).astype(BF16)


def _later_sums(log_rems, tri):
    m = log_rems[0].shape[0]
    parts = []
    for lr in log_rems:
        parts += list(_split_hi_lo(lr))
    res = _dot(jnp.concatenate(parts, axis=0), tri)
    return [res[2 * i * m:(2 * i + 1) * m] + res[(2 * i + 1) * m:(2 * i + 2) * m] for i in range(len(log_rems))]


def _sb_prompt_body(q_ref, k_ref, v_ref, o_ref, acc_s, run_s):
    qi = pl.program_id(1)
    D = SB_HEAD_DIM
    row = lax.broadcasted_iota(jnp.int32, (SB_QBLK, SB_KBLK), 0)
    col = lax.broadcasted_iota(jnp.int32, (SB_QBLK, SB_KBLK), 1)
    tri = _later_key_matrix(SB_KBLK)
    acc_s[...] = jnp.zeros(acc_s.shape, F32)
    run_s[...] = jnp.zeros(run_s.shape, F32)
    n_kb = (qi + 1) * (SB_QBLK // SB_KBLK)

    def kb_body(i, carry):
        off = pl.multiple_of((n_kb - 1 - i) * SB_KBLK, SB_KBLK)
        valid = (off + col) < (qi * SB_QBLK + row)
        for h0 in range(0, SB_HEADS, SB_HEAD_GROUP):
            heads = range(h0, h0 + SB_HEAD_GROUP)
            scores = []
            for h in heads:
                hs = slice(h * D, (h + 1) * D)
                qh = (q_ref[0, :, hs] * SB_SCALE).astype(BF16)
                kh = k_ref[0, pl.ds(off, SB_KBLK), hs].astype(BF16)
                scores.append(_sb_scores(_dot_nt(qh, kh), valid))
            afters = _later_sums([sc[1] for sc in scores], tri)
            for h, (log_beta, log_rem), after in zip(heads, scores, afters):
                hs = slice(h * D, (h + 1) * D)
                a = jnp.where(valid, jnp.exp(log_beta + after + run_s[h]), 0.0).astype(BF16)
                acc_s[h] = acc_s[h] + _dot(a, v_ref[0, pl.ds(off, SB_KBLK), hs].astype(BF16))
                run_s[h] = run_s[h] + jnp.sum(log_rem, axis=-1, keepdims=True)
        return carry

    lax.fori_loop(0, n_kb, kb_body, 0)
    for h in range(SB_HEADS):
        o_ref[0, :, h * D:(h + 1) * D] = acc_s[h]


def _sb_prompt(q, k, v):
    B, T, W = q.shape
    assert T % SB_QBLK == 0
    return pl.pallas_call(
        _sb_prompt_body,
        out_shape=jax.ShapeDtypeStruct((B, T, W), F32),
        grid=(B, T // SB_QBLK),
        in_specs=[pl.BlockSpec((1, SB_QBLK, W), lambda b, i: (b, i, 0)),
                  pl.BlockSpec((1, T, W), lambda b, i: (b, 0, 0)),
                  pl.BlockSpec((1, T, W), lambda b, i: (b, 0, 0))],
        out_specs=pl.BlockSpec((1, SB_QBLK, W), lambda b, i: (b, i, 0)),
        scratch_shapes=[pltpu.VMEM((SB_HEADS, SB_QBLK, SB_HEAD_DIM), F32), pltpu.VMEM((SB_HEADS, SB_QBLK, 1), F32)],
        compiler_params=_cparams("parallel", "arbitrary"),
        name="sb_prompt",
    )(q, k, v)


def _sb_paged_body(pt_ref, q_ref, kn_ref, vn_ref, *refs, n_pg):
    k_pages, v_pages = refs[:n_pg], refs[n_pg:2 * n_pg]
    o_ref, q_s, acc_s, run_s = refs[2 * n_pg:]
    g = pl.program_id(1)
    D, T, W = SB_HEAD_DIM, q_ref.shape[1], SB_WIDTH
    rows = SB_HEADS * T
    assert n_pg % 2 == 0

    @pl.when(g == 0)
    def _():
        tri = _later_key_matrix(PAGE_SIZE)
        q_s[...] = jnp.zeros(q_s.shape, F32)
        for h in range(SB_HEADS):
            q_s[h * T:(h + 1) * T, h * D:(h + 1) * D] = q_ref[0, :, h * D:(h + 1) * D] * SB_SCALE
        t_pos = lax.broadcasted_iota(jnp.int32, (rows, PAGE_SIZE), 0) % T
        col = lax.broadcasted_iota(jnp.int32, (rows, PAGE_SIZE), 1)
        valid = col < t_pos
        log_beta, log_rem = _sb_scores(_dot_nt(q_s[...].astype(BF16), kn_ref[0].astype(BF16)), valid)
        a, run = _sb_weights(log_beta, log_rem, valid, tri, jnp.zeros((rows, 1), F32))
        acc_s[...] = _dot(a, vn_ref[0].astype(BF16))
        run_s[...] = run

    qbd = q_s[...].astype(BF16)
    k_all = jnp.concatenate([k_pages[p][...].reshape(W, PAGE_SIZE) for p in range(n_pg)], axis=1).astype(BF16)
    v_all = jnp.concatenate([v_pages[p][...].reshape(W, PAGE_SIZE) for p in range(n_pg)], axis=1).astype(BF16)
    log_beta, log_rem = _sb_scores(_dot(qbd, k_all), None)
    pair = 2 * PAGE_SIZE
    n_pairs = n_pg // 2
    afters = _later_sums([log_rem[:, i * pair:(i + 1) * pair] for i in range(n_pairs)], _pair_later_matrix())
    run = run_s[...]
    weights = []
    for i in range(n_pairs):
        weights.append(jnp.exp(log_beta[:, i * pair:(i + 1) * pair] + afters[i] + run).astype(BF16))
        run = run + jnp.sum(log_rem[:, i * pair:(i + 1) * pair], axis=-1, keepdims=True)
    acc = acc_s[...] + _dot_nt(jnp.concatenate(weights, axis=1), v_all)
    acc_s[...] = acc
    run_s[...] = run

    @pl.when(g == pl.num_programs(1) - 1)
    def _():
        for h in range(SB_HEADS):
            o_ref[0, :, h * D:(h + 1) * D] = acc_s[h * T:(h + 1) * T, h * D:(h + 1) * D]


def _sb_paged(q, k_new, v_new, cache_kt, cache_vt, page_table, j):
    B, T, W = q.shape
    n_pages = page_table.shape[1]
    n_pg = PAGES_PER_STEP
    assert n_pages % n_pg == 0
    rows = SB_HEADS * T

    def page_spec(p):
        return pl.BlockSpec((None, None, SB_HEADS, SB_HEAD_DIM, PAGE_SIZE),
                            lambda b, g, pt: (pt[b, n_pages - 1 - (g * n_pg + p)], j, 0, 0, 0))

    new_spec = pl.BlockSpec((1, PAGE_SIZE, W), lambda b, g, pt: (b, 0, 0))
    row_spec = pl.BlockSpec((1, T, W), lambda b, g, pt: (b, 0, 0))
    return pl.pallas_call(
        functools.partial(_sb_paged_body, n_pg=n_pg),
        out_shape=jax.ShapeDtypeStruct((B, T, W), F32),
        grid_spec=pltpu.PrefetchScalarGridSpec(
            num_scalar_prefetch=1, grid=(B, n_pages // n_pg),
            in_specs=[row_spec, new_spec, new_spec] + [page_spec(p) for p in range(n_pg)] * 2,
            out_specs=row_spec,
            scratch_shapes=[pltpu.VMEM((rows, W), F32), pltpu.VMEM((rows, W), F32), pltpu.VMEM((rows, 1), F32)]),
        compiler_params=_cparams("parallel", "arbitrary"),
        name="sb_paged",
    )(page_table, q, k_new, v_new, *([cache_kt] * n_pg), *([cache_vt] * n_pg))


def _rope_lanes(x, cos, sin):
    W = x.shape[-1]
    lane = lax.broadcasted_iota(jnp.int32, x.shape, 1)
    first = (lane % MLA_ROPE) < ROPE_HALF
    from_right = pltpu.roll(x, W - ROPE_HALF, 1)
    from_left = pltpu.roll(x, ROPE_HALF, 1)
    return x * cos + jnp.where(first, -from_right, from_left) * sin


def _mla_prep_body(qd_ref, kvd_ref, kr_ref, cq_ref, sq_ref, ck_ref, sk_ref, qg_ref, wq_ref, kvg_ref, wuk_ref,
                   ql_ref, qp_ref, ckv_ref, kpe_ref, *, sb, tb):
    rows = sb * tb
    nope_w = MLA_HEADS * MLA_NOPE

    def table(ref):
        t = ref[...]
        return jnp.broadcast_to(t[None], (sb,) + t.shape).reshape(rows, t.shape[-1])

    qd = qd_ref[...]
    qn = qd * lax.rsqrt(jnp.mean(qd * qd, axis=-1, keepdims=True) + NORM_EPS) * qg_ref[...]
    q = _dot(qn.reshape(rows, MLA_Q_RANK).astype(BF16), wq_ref[...])
    q_pe = _rope_lanes(q[:, nope_w:], table(cq_ref), table(sq_ref))
    qp_ref[...] = q_pe.reshape(sb, tb, MLA_HEADS * MLA_ROPE)
    for h in range(MLA_HEADS):
        q_lat = _dot(q[:, h * MLA_NOPE:(h + 1) * MLA_NOPE].astype(BF16), wuk_ref[h])
        ql_ref[:, :, h * MLA_KV_RANK:(h + 1) * MLA_KV_RANK] = q_lat.reshape(sb, tb, MLA_KV_RANK)
    kvd = kvd_ref[...]
    ckv_ref[...] = kvd * lax.rsqrt(jnp.mean(kvd * kvd, axis=-1, keepdims=True) + NORM_EPS) * kvg_ref[...]
    kpe = _rope_lanes(kr_ref[...].reshape(rows, KR_PAD), table(ck_ref), table(sk_ref))
    kpe_ref[...] = kpe.reshape(sb, tb, KR_PAD)


def _mla_prep(qd, kvd, kr, tables, prm, rows=256):
    S, T, _ = qd.shape
    sb, tb = _blocking(S, T, rows)
    cos_q, sin_q, cos_k, sin_k = tables

    def row_spec(width):
        return pl.BlockSpec((sb, tb, width), lambda i, j: (i, j, 0))

    def tab_spec(width):
        return pl.BlockSpec((tb, width), lambda i, j: (j, 0))

    def full_spec(arr):
        return pl.BlockSpec(arr.shape, lambda i, j: (0,) * arr.ndim)

    widths = (MLA_HEADS * MLA_KV_RANK, MLA_HEADS * MLA_ROPE, MLA_KV_RANK, KR_PAD)
    return pl.pallas_call(
        functools.partial(_mla_prep_body, sb=sb, tb=tb),
        out_shape=[jax.ShapeDtypeStruct((S, T, n), F32) for n in widths],
        grid=(S // sb, T // tb),
        in_specs=[row_spec(MLA_Q_RANK), row_spec(MLA_KV_RANK), row_spec(KR_PAD),
                  tab_spec(cos_q.shape[1]), tab_spec(cos_q.shape[1]), tab_spec(KR_PAD), tab_spec(KR_PAD),
                  full_spec(prm["q_g"]), full_spec(prm["w_q"]), full_spec(prm["kv_g"]), full_spec(prm["w_uk"])],
        out_specs=[row_spec(n) for n in widths],
        compiler_params=_cparams("parallel", "parallel"),
        name="mla_prep",
    )(qd, kvd, kr, cos_q, sin_q, cos_k, sin_k, prm["q_g"], prm["w_q"], prm["kv_g"], prm["w_uk"])


def _softmax_step(s, c, m_s, l_s, acc_s):
    m_old = m_s[...]
    m_new = jnp.maximum(m_old, jnp.max(s, axis=-1, keepdims=True))
    alpha = jnp.exp(m_old - m_new)
    p = jnp.exp(s - m_new)
    l_s[...] = alpha * l_s[...] + jnp.sum(p, axis=-1, keepdims=True)
    acc_s[...] = alpha * acc_s[...] + _dot(p.astype(BF16), c)
    m_s[...] = m_new


def _stack_heads(ql_ref, qp_ref, ql_s, qp_s, T):
    for h in range(MLA_HEADS):
        ql_s[h * T:(h + 1) * T, :] = ql_ref[0, :, h * MLA_KV_RANK:(h + 1) * MLA_KV_RANK].astype(ql_s.dtype)
        qp_s[h * T:(h + 1) * T, :] = qp_ref[0, :, h * MLA_ROPE:(h + 1) * MLA_ROPE].astype(qp_s.dtype)


def _mla_finish(o_ref, wuv_ref, l_s, acc_s, T):
    o_lat = acc_s[...] / l_s[...]
    for h in range(MLA_HEADS):
        o_ref[0, :, h * MLA_V:(h + 1) * MLA_V] = _dot(o_lat[h * T:(h + 1) * T].astype(BF16), wuv_ref[h])


MLA_ROW_GROUPS = 4


def _mla_prompt_body(ql_ref, qp_ref, ckv_ref, kpe_ref, wuv_ref, o_ref, ql_s, qp_s, m_s, l_s, acc_s):
    qi = pl.program_id(1)
    G = MLA_ROW_GROUPS
    hpg = MLA_HEADS // G
    grows = hpg * QBLK
    for h in range(MLA_HEADS):
        g, r0 = h // hpg, (h % hpg) * QBLK
        ql_s[g, r0:r0 + QBLK, :] = ql_ref[0, :, h * MLA_KV_RANK:(h + 1) * MLA_KV_RANK].astype(BF16)
        qp_s[g, r0:r0 + QBLK, :] = qp_ref[0, :, h * MLA_ROPE:(h + 1) * MLA_ROPE].astype(BF16)
    m_s[...] = jnp.full(m_s.shape, -jnp.inf, F32)
    l_s[...] = jnp.zeros(l_s.shape, F32)
    acc_s[...] = jnp.zeros(acc_s.shape, F32)
    q_pos = qi * QBLK + lax.broadcasted_iota(jnp.int32, (grows, MLA_KBLK), 0) % QBLK
    col = lax.broadcasted_iota(jnp.int32, (grows, MLA_KBLK), 1)

    def kb_body(kb, carry):
        off = pl.multiple_of(kb * MLA_KBLK, MLA_KBLK)
        c = ckv_ref[0, pl.ds(off, MLA_KBLK), :].astype(BF16)
        pe = kpe_ref[0, pl.ds(off, MLA_KBLK), 0:MLA_ROPE].astype(BF16)
        valid = off + col <= q_pos
        for g in range(G):
            s = (_dot_nt(ql_s[g], c) + _dot_nt(qp_s[g], pe)) * MLA_SCALE
            _softmax_step(jnp.where(valid, s, -jnp.inf), c, m_s.at[g], l_s.at[g], acc_s.at[g])
        return carry

    n_kb = (qi * QBLK + QBLK + MLA_KBLK - 1) // MLA_KBLK
    lax.fori_loop(0, n_kb, kb_body, 0)
    for h in range(MLA_HEADS):
        g, r0 = h // hpg, (h % hpg) * QBLK
        o_lat = acc_s[g, r0:r0 + QBLK, :] / l_s[g, r0:r0 + QBLK, :]
        o_ref[0, :, h * MLA_V:(h + 1) * MLA_V] = _dot(o_lat.astype(BF16), wuv_ref[h])


def _mla_prompt(q_lat, q_pe, ckv, kpe, w_uv):
    B, T, _ = q_lat.shape
    assert T % MLA_KBLK == 0
    G = MLA_ROW_GROUPS
    grows = MLA_HEADS // G * QBLK
    return pl.pallas_call(
        _mla_prompt_body,
        out_shape=jax.ShapeDtypeStruct((B, T, MLA_WIDTH), F32),
        grid=(B, T // QBLK),
        in_specs=[pl.BlockSpec((1, QBLK, q_lat.shape[2]), lambda b, i: (b, i, 0)),
                  pl.BlockSpec((1, QBLK, q_pe.shape[2]), lambda b, i: (b, i, 0)),
                  pl.BlockSpec((1, T, MLA_KV_RANK), lambda b, i: (b, 0, 0)),
                  pl.BlockSpec((1, T, KR_PAD), lambda b, i: (b, 0, 0)),
                  pl.BlockSpec(w_uv.shape, lambda b, i: (0, 0, 0))],
        out_specs=pl.BlockSpec((1, QBLK, MLA_WIDTH), lambda b, i: (b, i, 0)),
        scratch_shapes=[pltpu.VMEM((G, grows, MLA_KV_RANK), BF16), pltpu.VMEM((G, grows, MLA_ROPE), BF16),
                        pltpu.VMEM((G, grows, 1), F32), pltpu.VMEM((G, grows, 1), F32),
                        pltpu.VMEM((G, grows, MLA_KV_RANK), F32)],
        compiler_params=_cparams("parallel", "arbitrary"),
        name="mla_prompt",
    )(q_lat, q_pe, ckv, kpe, w_uv)


def _mla_paged_body(pt_ref, ql_ref, qp_ref, cn_ref, pn_ref, *refs, n_pg):
    c_pages, p_pages = refs[:n_pg], refs[n_pg:2 * n_pg]
    wuv_ref, o_ref, ql_s, qp_s, m_s, l_s, acc_s = refs[2 * n_pg:]
    g = pl.program_id(1)
    T = ql_ref.shape[1]
    rows = MLA_HEADS * T

    @pl.when(g == 0)
    def _():
        _stack_heads(ql_ref, qp_ref, ql_s, qp_s, T)
        m_s[...] = jnp.full(m_s.shape, -jnp.inf, F32)
        l_s[...] = jnp.zeros(l_s.shape, F32)
        acc_s[...] = jnp.zeros(acc_s.shape, F32)
        t_pos = lax.broadcasted_iota(jnp.int32, (rows, PAGE_SIZE), 0) % T
        col = lax.broadcasted_iota(jnp.int32, (rows, PAGE_SIZE), 1)
        c = cn_ref[0].astype(BF16)
        pe = pn_ref[0, :, 0:MLA_ROPE].astype(BF16)
        s = (_dot_nt(ql_s[...].astype(BF16), c) + _dot_nt(qp_s[...].astype(BF16), pe)) * MLA_SCALE
        _softmax_step(jnp.where(col <= t_pos, s, -jnp.inf), c, m_s, l_s, acc_s)

    q_lat, q_pe = ql_s[...].astype(BF16), qp_s[...].astype(BF16)
    c_all = jnp.concatenate([c_pages[p][...] for p in range(n_pg)], axis=0).astype(BF16)
    pe_all = jnp.concatenate([p_pages[p][...] for p in range(n_pg)], axis=1).astype(BF16)
    s = (_dot_nt(q_lat, c_all) + _dot(q_pe, pe_all)) * MLA_SCALE
    m_old = m_s[...]
    m_new = jnp.maximum(m_old, jnp.max(s, axis=-1, keepdims=True))
    alpha = jnp.exp(m_old - m_new)
    e = jnp.exp(s - m_new)
    l = alpha * l_s[...] + jnp.sum(e, axis=-1, keepdims=True)
    acc = alpha * acc_s[...] + _dot(e.astype(BF16), c_all)
    m_s[...] = m_new
    l_s[...] = l
    acc_s[...] = acc

    @pl.when(g == pl.num_programs(1) - 1)
    def _():
        _mla_finish(o_ref, wuv_ref, l_s, acc_s, T)


def _mla_paged(q_lat, q_pe, ckv_new, kpe_new, cache_ckv, cache_kpe_t, page_table, j, w_uv):
    B, T, _ = q_lat.shape
    n_pages = page_table.shape[1]
    n_pg = PAGES_PER_STEP
    assert n_pages % n_pg == 0
    rows = MLA_HEADS * T

    def page_spec(shape, p):
        return pl.BlockSpec((None, None) + shape, lambda b, g, pt: (pt[b, g * n_pg + p], j, 0, 0))

    def seq_spec(t, width):
        return pl.BlockSpec((1, t, width), lambda b, g, pt: (b, 0, 0))

    return pl.pallas_call(
        functools.partial(_mla_paged_body, n_pg=n_pg),
        out_shape=jax.ShapeDtypeStruct((B, T, MLA_WIDTH), F32),
        grid_spec=pltpu.PrefetchScalarGridSpec(
            num_scalar_prefetch=1, grid=(B, n_pages // n_pg),
            in_specs=[seq_spec(T, q_lat.shape[2]), seq_spec(T, q_pe.shape[2]),
                      seq_spec(PAGE_SIZE, MLA_KV_RANK), seq_spec(PAGE_SIZE, KR_PAD)]
            + [page_spec((PAGE_SIZE, MLA_KV_RANK), p) for p in range(n_pg)]
            + [page_spec((MLA_ROPE, PAGE_SIZE), p) for p in range(n_pg)]
            + [pl.BlockSpec(w_uv.shape, lambda b, g, pt: (0, 0, 0))],
            out_specs=seq_spec(T, MLA_WIDTH),
            scratch_shapes=[pltpu.VMEM((rows, MLA_KV_RANK), F32), pltpu.VMEM((rows, MLA_ROPE), F32),
                            pltpu.VMEM((rows, 1), F32), pltpu.VMEM((rows, 1), F32),
                            pltpu.VMEM((rows, MLA_KV_RANK), F32)]),
        compiler_params=_cparams("parallel", "arbitrary"),
        name="mla_paged",
    )(page_table, q_lat, q_pe, ckv_new, kpe_new, *([cache_ckv] * n_pg), *([cache_kpe_t] * n_pg), w_uv)


S5_SEQS = 8


def _s5_body(u_ref, re0_ref, im0_ref, are_ref, aim_ref, bre_ref, bim_ref, cre_ref, cim_ref, d_ref, wglu_ref,
             bglu_ref, o_ref, re_out_ref, im_out_ref, xre_s, xim_s, re_s, im_s, *, tc):
    tj = pl.program_id(1)
    rows = tc * S5_SEQS

    @pl.when(tj == 0)
    def _():
        re_s[...] = re0_ref[...]
        im_s[...] = im0_ref[...]

    u = u_ref[...].reshape(rows, S5_WIDTH)
    ub = u.astype(BF16)
    xre_s[...] = _dot(ub, bre_ref[...]).reshape(tc, S5_SEQS, S5_FLAT)
    xim_s[...] = _dot(ub, bim_ref[...]).reshape(tc, S5_SEQS, S5_FLAT)
    a_re, a_im = are_ref[...], aim_ref[...]

    def step(t, carry):
        re, im = carry
        n_re = a_re * re - a_im * im + xre_s[t]
        n_im = a_re * im + a_im * re + xim_s[t]
        xre_s[t] = n_re
        xim_s[t] = n_im
        return n_re, n_im

    re, im = lax.fori_loop(0, tc, step, (re_s[...], im_s[...]))
    re_s[...] = re
    im_s[...] = im
    x_re = xre_s[...].reshape(rows, S5_FLAT).astype(BF16)
    x_im = xim_s[...].reshape(rows, S5_FLAT).astype(BF16)
    y = _dot(x_re, cre_ref[...]) - _dot(x_im, cim_ref[...]) + d_ref[...] * u
    y = 0.5 * y * (1.0 + lax.erf(y * (1.0 / math.sqrt(2.0))))
    out = y * jax.nn.sigmoid(_dot(y.astype(BF16), wglu_ref[...]) + bglu_ref[...])
    o_ref[...] = out.reshape(tc, S5_SEQS, S5_WIDTH)

    @pl.when(tj == pl.num_programs(1) - 1)
    def _():
        re_out_ref[...] = re
        im_out_ref[...] = im


def _s5(u_t, re0, im0, prm, tc):
    T, S, _ = u_t.shape
    assert S % S5_SEQS == 0 and T % tc == 0
    names = ("a_re", "a_im", "b_re", "b_im", "c_re", "c_im", "d", "w_glu", "b_glu")
    state_spec = pl.BlockSpec((S5_SEQS, S5_FLAT), lambda i, j: (i, 0))
    row_spec = pl.BlockSpec((tc, S5_SEQS, S5_WIDTH), lambda i, j: (j, i, 0))
    return pl.pallas_call(
        functools.partial(_s5_body, tc=tc),
        out_shape=[jax.ShapeDtypeStruct((T, S, S5_WIDTH), F32), jax.ShapeDtypeStruct((S, S5_FLAT), F32),
                   jax.ShapeDtypeStruct((S, S5_FLAT), F32)],
        grid=(S // S5_SEQS, T // tc),
        in_specs=[row_spec, state_spec, state_spec]
        + [pl.BlockSpec(prm[n].shape, lambda i, j: (0, 0)) for n in names],
        out_specs=[row_spec, state_spec, state_spec],
        scratch_shapes=[pltpu.VMEM((tc, S5_SEQS, S5_FLAT), F32)] * 2 + [pltpu.VMEM((S5_SEQS, S5_FLAT), F32)] * 2,
        compiler_params=_cparams("parallel", "arbitrary"),
        name="s5",
    )(u_t, re0, im0, *(prm[n] for n in names))


def _router_body(x_ref, g_ref, sc_ref, sh_ref, wr_ref, br_ref, h_ref, comb_ref, *, sb, tb):
    rows = sb * tb
    x = x_ref[...]
    h = x * lax.rsqrt(jnp.mean(x * x, axis=-1, keepdims=True) + NORM_EPS) * g_ref[...]
    h = h * (1.0 + sc_ref[...]) + sh_ref[...]
    h_ref[...] = h
    logits = jnp.dot(h.reshape(rows, D_MODEL), wr_ref[...], precision=HIGHEST,
                     preferred_element_type=F32) + br_ref[...]
    lane = lax.broadcasted_iota(jnp.int32, (rows, N_EXPERTS), 1)
    work = logits
    vals, sels = [], []
    for _ in range(TOP_K):
        m = jnp.max(work, axis=-1, keepdims=True)
        idx = jnp.min(jnp.where(work == m, lane, N_EXPERTS), axis=-1, keepdims=True)
        sel = lane == idx
        vals.append(m)
        sels.append(sel)
        work = jnp.where(sel, -jnp.inf, work)
    exps = [jnp.exp(v - vals[0]) for v in vals]
    denom = exps[0] + exps[1] + exps[2] + exps[3]
    comb = jnp.zeros((rows, N_EXPERTS), F32)
    for e, sel in zip(exps, sels):
        comb = jnp.where(sel, e / denom, comb)
    comb_ref[...] = comb.reshape(sb, tb, N_EXPERTS)


def _router(x, g, scale, shift, w_r, b_r, rows=512):
    S, T, D = x.shape
    sb, tb = _blocking(S, T, rows)
    row = pl.BlockSpec((sb, tb, D), lambda i, j: (i, j, 0))
    seq = pl.BlockSpec((sb, 1, D), lambda i, j: (i, 0, 0))
    return pl.pallas_call(
        functools.partial(_router_body, sb=sb, tb=tb),
        out_shape=[jax.ShapeDtypeStruct((S, T, D), F32), jax.ShapeDtypeStruct((S, T, N_EXPERTS), F32)],
        grid=(S // sb, T // tb),
        in_specs=[row, pl.BlockSpec((1, D), lambda i, j: (0, 0)), seq, seq,
                  pl.BlockSpec((D, N_EXPERTS), lambda i, j: (0, 0)),
                  pl.BlockSpec((1, N_EXPERTS), lambda i, j: (0, 0))],
        out_specs=[row, pl.BlockSpec((sb, tb, N_EXPERTS), lambda i, j: (i, j, 0))],
        compiler_params=_cparams("parallel", "parallel"),
        name="router",
    )(x, g.reshape(1, D), scale, shift, w_r, b_r.reshape(1, N_EXPERTS))


def _experts_body(h_ref, comb_ref, x_ref, gate_ref, wg_ref, bg_ref, wl_ref, bl_ref, wd_ref, bd_ref, o_ref,
                  hb_s, acc_s, *, sb, tb):
    step = pl.program_id(2)
    rows = sb * tb

    @pl.when(step == 0)
    def _():
        hb_s[...] = h_ref[...].reshape(rows, D_MODEL).astype(BF16)
        acc_s[...] = jnp.zeros(acc_s.shape, F32)

    hb = hb_s[...]
    comb = comb_ref[...].reshape(rows, N_EXPERTS)
    lane = lax.broadcasted_iota(jnp.int32, (rows, N_EXPERTS), 1)
    total = None
    for i in range(EXPERTS_PER_STEP):
        glu = jnp.minimum(_dot(hb, wg_ref[i]) + bg_ref[i], SWIGLU_LIMIT)
        lin = jnp.clip(_dot(hb, wl_ref[i]) + bl_ref[i], -SWIGLU_LIMIT, SWIGLU_LIMIT)
        act = glu * jax.nn.sigmoid(SWIGLU_ALPHA * glu) * (lin + 1.0)
        y = _dot(act.astype(BF16), wd_ref[i]) + bd_ref[i]
        w_e = jnp.sum(jnp.where(lane == step * EXPERTS_PER_STEP + i, comb, 0.0), axis=-1, keepdims=True)
        total = w_e * y if total is None else total + w_e * y
    acc_s[...] = acc_s[...] + total

    @pl.when(step == pl.num_programs(2) - 1)
    def _():
        o_ref[...] = x_ref[...] + gate_ref[...] * acc_s[...].reshape(sb, tb, D_MODEL)


def _experts(h, comb, x, gate, prm, rows=1024):
    S, T, D = x.shape
    sb, tb = _blocking(S, T, rows)
    F, EP = D_FF_EXPERT, EXPERTS_PER_STEP
    row = pl.BlockSpec((sb, tb, D), lambda i, j, e: (i, j, 0))
    return pl.pallas_call(
        functools.partial(_experts_body, sb=sb, tb=tb),
        out_shape=jax.ShapeDtypeStruct((S, T, D), F32),
        grid=(S // sb, T // tb, N_EXPERTS // EP),
        in_specs=[row, pl.BlockSpec((sb, tb, N_EXPERTS), lambda i, j, e: (i, j, 0)), row,
                  pl.BlockSpec((sb, 1, D), lambda i, j, e: (i, 0, 0)),
                  pl.BlockSpec((EP, D, F), lambda i, j, e: (e, 0, 0)),
                  pl.BlockSpec((EP, 1, F), lambda i, j, e: (e, 0, 0)),
                  pl.BlockSpec((EP, D, F), lambda i, j, e: (e, 0, 0)),
                  pl.BlockSpec((EP, 1, F), lambda i, j, e: (e, 0, 0)),
                  pl.BlockSpec((EP, F, D), lambda i, j, e: (e, 0, 0)),
                  pl.BlockSpec((EP, 1, D), lambda i, j, e: (e, 0, 0))],
        out_specs=row,
        scratch_shapes=[pltpu.VMEM((sb * tb, D), BF16), pltpu.VMEM((sb * tb, D), F32)],
        compiler_params=_cparams("parallel", "parallel", "arbitrary"),
        name="experts",
    )(h, comb, x, gate, prm["w_g"], prm["b_g"], prm["w_l"], prm["b_l"], prm["w_d"], prm["b_d"])


def _rw_column_order():
    W, dl, al = RW_WIDTH, RW_DECAY_LORA, RW_AAA_LORA
    r = np.arange(0, W)
    wd = np.arange(W, W + dl)
    k = np.arange(W + dl, 2 * W + dl)
    v = np.arange(2 * W + dl, 3 * W + dl)
    rest = np.arange(3 * W + dl, RW_PROJ)
    return np.concatenate([r, k, v, wd, rest])


def _q_up_column_order():
    per = MLA_NOPE + MLA_ROPE
    nope = np.concatenate([np.arange(h * per, h * per + MLA_NOPE) for h in range(MLA_HEADS)])
    rope = np.concatenate([np.arange(h * per + MLA_NOPE, (h + 1) * per) for h in range(MLA_HEADS)])
    return np.concatenate([nope, rope])


def _block_diag(blocks):
    G, a, b = blocks.shape
    eye = jnp.eye(G, dtype=blocks.dtype)
    return (eye[:, None, :, None] * blocks[:, :, None, :]).reshape(G * a, G * b)


def _even_params(P, j):
    order = _rw_column_order()
    w_in = P["ev_w_in"][j]
    row = lambda a: a.reshape(1, -1).astype(F32)
    return dict(
        w_in=jnp.concatenate([w_in[:, :RW_PROJ][:, order], w_in[:, RW_PROJ:]], axis=1).astype(BF16),
        w_out_rw=P["ev_w_out"][j][:RW_WIDTH].astype(BF16),
        w_out_sb=P["ev_w_out"][j][RW_WIDTH:].astype(BF16),
        rw=dict(mu=row(P["rw_mu"][j][order]), w0=row(P["rw_w0"][j]), w2=P["rw_w2"][j].astype(BF16),
                a0=row(P["rw_a0"][j]), a2=P["rw_a2"][j].astype(BF16), g2=P["rw_g2"][j].astype(BF16),
                k_k=row(P["rw_k_k"][j]), k_a=row(P["rw_k_a"][j]), r_k=row(P["rw_r_k"][j]),
                ln_w=row(P["rw_ln_w"][j]), ln_b=row(P["rw_ln_b"][j])))


def _odd_params(P, j):
    row = lambda a: a.reshape(1, -1).astype(F32)
    w_in = P["od_w_in"][j]
    pad = jnp.zeros((D_MODEL, KR_PAD - MLA_ROPE), w_in.dtype)
    lr, li = P["s5_lam_re"][j].astype(F32), P["s5_lam_im"][j].astype(F32)
    dt = jnp.exp(P["s5_log_step"][j].astype(F32))[:, None]
    mag = jnp.exp(lr * dt)
    ab_re, ab_im = mag * jnp.cos(li * dt), mag * jnp.sin(li * dt)
    den = lr * lr + li * li
    f_re = ((ab_re - 1.0) * lr + ab_im * li) / den
    f_im = (ab_im * lr - (ab_re - 1.0) * li) / den
    b_re, b_im = P["s5_b_re"][j].astype(F32), P["s5_b_im"][j].astype(F32)
    bb_re = f_re[..., None] * b_re - f_im[..., None] * b_im
    bb_im = f_re[..., None] * b_im + f_im[..., None] * b_re
    to_in = lambda a: _block_diag(jnp.swapaxes(a, 1, 2)).astype(BF16)
    to_out = lambda a: _block_diag(jnp.swapaxes(a, 1, 2)).astype(BF16)
    return dict(
        w_in=jnp.concatenate([w_in, pad], axis=1).astype(BF16),
        w_out_s5=P["od_w_out"][j][:S5_WIDTH].astype(BF16),
        w_out_mla=P["od_w_out"][j][S5_WIDTH:].astype(BF16),
        s5=dict(a_re=row(ab_re), a_im=row(ab_im), b_re=to_in(bb_re), b_im=to_in(bb_im),
                c_re=to_out(P["s5_c_re"][j].astype(F32)), c_im=to_out(P["s5_c_im"][j].astype(F32)),
                d=row(P["s5_d"][j]), w_glu=P["s5_w_glu"][j].astype(BF16), b_glu=row(P["s5_b_glu"][j])),
        mla=dict(q_g=row(P["mla_q_norm_g"][j]), w_q=P["mla_w_q_up"][j][:, _q_up_column_order()].astype(BF16),
                 kv_g=row(P["mla_kv_norm_g"][j]),
                 w_uk=jnp.transpose(P["mla_w_uk"][j], (1, 2, 0)).astype(BF16),
                 w_uv=jnp.transpose(P["mla_w_uv"][j], (1, 0, 2)).astype(BF16)))


def _deinterleave_body(w_ref, sel_ref, even_ref, odd_ref):
    half = even_ref.shape[-1]
    y = _dot(w_ref[0].astype(BF16), sel_ref[...])
    even_ref[0] = y[:, :half].astype(BF16)
    odd_ref[0] = y[:, half:].astype(BF16)


def _deinterleave(w):
    E, D, F2 = w.shape
    F = F2 // 2
    col = np.arange(F2)
    sel = (np.arange(F2)[:, None] == (2 * (col % F) + col // F)[None, :]).astype(np.float32)
    return pl.pallas_call(
        _deinterleave_body,
        out_shape=[jax.ShapeDtypeStruct((E, D, F), BF16)] * 2,
        grid=(E,),
        in_specs=[pl.BlockSpec((1, D, F2), lambda e: (e, 0, 0)), pl.BlockSpec((F2, F2), lambda e: (0, 0))],
        out_specs=[pl.BlockSpec((1, D, F), lambda e: (e, 0, 0))] * 2,
        compiler_params=_cparams("parallel"),
        name="deinterleave",
    )(w, jnp.asarray(sel, BF16))


def _moe_params(P, l):
    w_gu, b_gu = P["moe_w_gu"][l], P["moe_b_gu"][l]
    w_g, w_l = _deinterleave(w_gu)
    return dict(w_r=P["moe_w_router"][l].astype(F32), b_r=P["moe_b_router"][l].astype(F32),
                w_g=w_g, w_l=w_l,
                b_g=b_gu[:, None, 0::2].astype(F32), b_l=b_gu[:, None, 1::2].astype(F32),
                w_d=P["moe_w_down"][l].astype(BF16), b_d=P["moe_b_down"][l][:, None, :].astype(F32))


def _rope_tables(q_pos):
    inv = jnp.power(ROPE_THETA, -jnp.arange(ROPE_HALF, dtype=F32) / ROPE_HALF)
    ang = q_pos.astype(F32)[:, None] * inv[None, :]
    cos, sin = jnp.cos(ang), jnp.sin(ang)
    q_rep = MLA_HEADS * MLA_ROPE // ROPE_HALF
    k_rep = KR_PAD // ROPE_HALF
    return jnp.tile(cos, (1, q_rep)), jnp.tile(sin, (1, q_rep)), jnp.tile(cos, (1, k_rep)), jnp.tile(sin, (1, k_rep))


def _pad_rows(a, n):
    return jnp.pad(a, ((0, 0), (0, n - a.shape[1]), (0, 0)))


def _trunk(x, c, q_pos, shift0, wkv0, s5re0, s5im0, paged, P, prepared):
    S, T, D = x.shape
    new = {name: [] for name in ("sb_k", "sb_v", "ckv", "kpe", "wkv", "shift", "s5re", "s5im")}
    tables = _rope_tables(q_pos)
    if paged is not None:
        paged = dict(paged, sb_k=jnp.transpose(paged["sb_k"], (0, 1, 3, 4, 2)),
                     sb_v=jnp.transpose(paged["sb_v"], (0, 1, 3, 4, 2)),
                     kpe=jnp.transpose(paged["kpe"], (0, 1, 3, 2)))
    rw_sb, rw_tb = (S5_SEQS, T) if T < 256 else (4, 256)
    for l in range(DEPTH):
        j = l // 2
        mod = _linear(c[None], P["w_mod"][l].astype(BF16), b=P["b_mod"][l], silu=True, tn=1536)[0]
        sh1, sc1, g1, sh2, sc2, g2 = (mod[0, :, i * D:(i + 1) * D][:, None, :] for i in range(6))
        if l % 2 == 0:
            ep = prepared["even"][j]
            rw, q, k, v = _linear(x, ep["w_in"], norm_g=P["norm_mix_g"][l], scale=sc1, shift=sh1,
                                  splits=[RW_PROJ, SB_WIDTH, SB_WIDTH, SB_WIDTH], rows=256)
            shift = _normmod(x[:, T - 1:T], P["norm_mix_g"][l], sc1, sh1)
            if shift0 is None:
                prev = jnp.zeros((S, 1, RW_PROJ), F32)
            else:
                prev = _linear(shift0[:, j][None], ep["w_in"][:, :RW_PROJ])[0].reshape(S, 1, RW_PROJ)
            rw_out, wkv = _rwkv(rw, prev, None if wkv0 is None else wkv0[:, j], ep["rw"], sb=rw_sb, tb=rw_tb)
            if paged is None:
                sb_out = _sb_prompt(q, k, v)
            else:
                sb_out = _sb_paged(q, _pad_rows(k, PAGE_SIZE), _pad_rows(v, PAGE_SIZE), paged["sb_k"],
                                   paged["sb_v"], paged["page_table"], j)
            x = _linear(rw_out, ep["w_out_rw"], x2=sb_out, w2=ep["w_out_sb"], res=x, gate=g1)[0]
            new["shift"].append(shift[:, 0])
            new["wkv"].append(wkv)
            new["sb_k"].append(k.reshape(S, T, SB_HEADS, SB_HEAD_DIM))
            new["sb_v"].append(v.reshape(S, T, SB_HEADS, SB_HEAD_DIM))
        else:
            op = prepared["odd"][j]
            u, qd, kvd, kr = _linear(x, op["w_in"], norm_g=P["norm_mix_g"][l], scale=sc1, shift=sh1,
                                     splits=[S5_WIDTH, MLA_Q_RANK, MLA_KV_RANK, KR_PAD], rows=256)
            re0 = jnp.zeros((S, S5_FLAT), F32) if s5re0 is None else s5re0[:, j].reshape(S, S5_FLAT)
            im0 = jnp.zeros((S, S5_FLAT), F32) if s5im0 is None else s5im0[:, j].reshape(S, S5_FLAT)
            s5_t, s_re, s_im = _s5(jnp.swapaxes(u, 0, 1), re0, im0, op["s5"], tc=min(T, 128))
            s5_out = jnp.swapaxes(s5_t, 0, 1)
            q_lat, q_pe, ckv, kpe = _mla_prep(qd, kvd, kr, tables, op["mla"])
            if paged is None:
                mla_out = _mla_prompt(q_lat, q_pe, ckv, kpe, op["mla"]["w_uv"])
            else:
                mla_out = _mla_paged(q_lat, q_pe, _pad_rows(ckv, PAGE_SIZE), _pad_rows(kpe, PAGE_SIZE),
                                     paged["ckv"], paged["kpe"], paged["page_table"], j, op["mla"]["w_uv"])
            x = _linear(s5_out, op["w_out_s5"], x2=mla_out, w2=op["w_out_mla"], res=x, gate=g1)[0]
            new["s5re"].append(s_re.reshape(S, S5_GROUPS, S5_STATE))
            new["s5im"].append(s_im.reshape(S, S5_GROUPS, S5_STATE))
            new["ckv"].append(ckv)
            new["kpe"].append(kpe[..., :MLA_ROPE])
        mp = prepared["moe"][l]
        h, comb = _router(x, P["norm_ffn_g"][l], sc2, sh2, mp["w_r"], mp["b_r"])
        x = _experts(h, comb, x, g2, mp)
    y = _normmod(x, P["norm_final_g"])
    return y, {name: jnp.stack(vals, axis=1) for name, vals in new.items()}


def kernel(x_prompt, x_sample, cache_sb_k, cache_sb_v, cache_mla_ckv, cache_mla_kpe, state_rwkv_wkv,
           state_rwkv_shift, state_s5_re, state_s5_im, page_table, c_prompt, c_sample, w_mod, b_mod,
           norm_mix_g, norm_ffn_g, norm_final_g, ev_w_in, ev_w_out, rw_mu, rw_w0, rw_w2, rw_a0, rw_a2, rw_g2,
           rw_k_k, rw_k_a, rw_r_k, rw_ln_w, rw_ln_b, od_w_in, od_w_out, s5_lam_re, s5_lam_im, s5_b_re, s5_b_im,
           s5_c_re, s5_c_im, s5_d, s5_log_step, s5_w_glu, s5_b_glu, mla_q_norm_g, mla_w_q_up, mla_kv_norm_g,
           mla_w_uk, mla_w_uv, moe_w_router, moe_b_router, moe_w_gu, moe_b_gu, moe_w_down, moe_b_down):
    P = dict(w_mod=w_mod, b_mod=b_mod, norm_mix_g=norm_mix_g, norm_ffn_g=norm_ffn_g, norm_final_g=norm_final_g,
             ev_w_in=ev_w_in, ev_w_out=ev_w_out, rw_mu=rw_mu, rw_w0=rw_w0, rw_w2=rw_w2, rw_a0=rw_a0, rw_a2=rw_a2,
             rw_g2=rw_g2, rw_k_k=rw_k_k, rw_k_a=rw_k_a, rw_r_k=rw_r_k, rw_ln_w=rw_ln_w, rw_ln_b=rw_ln_b,
             od_w_in=od_w_in, od_w_out=od_w_out, s5_lam_re=s5_lam_re, s5_lam_im=s5_lam_im, s5_b_re=s5_b_re,
             s5_b_im=s5_b_im, s5_c_re=s5_c_re, s5_c_im=s5_c_im, s5_d=s5_d, s5_log_step=s5_log_step,
             s5_w_glu=s5_w_glu, s5_b_glu=s5_b_glu, mla_q_norm_g=mla_q_norm_g, mla_w_q_up=mla_w_q_up,
             mla_kv_norm_g=mla_kv_norm_g, mla_w_uk=mla_w_uk, mla_w_uv=mla_w_uv, moe_w_router=moe_w_router,
             moe_b_router=moe_b_router, moe_w_gu=moe_w_gu, moe_b_gu=moe_b_gu, moe_w_down=moe_w_down,
             moe_b_down=moe_b_down)
    n_even, n_odd = (DEPTH + 1) // 2, DEPTH // 2
    prepared = dict(even=[_even_params(P, j) for j in range(n_even)],
                    odd=[_odd_params(P, j) for j in range(n_odd)],
                    moe=[_moe_params(P, l) for l in range(DEPTH)])
    sp = x_prompt.shape[1]
    y_prompt, pn = _trunk(x_prompt, c_prompt, jnp.arange(sp, dtype=jnp.int32), None, None, None, None, None,
                          P, prepared)
    past_len = page_table.shape[1] * cache_sb_k.shape[2]
    paged = dict(page_table=page_table, sb_k=cache_sb_k, sb_v=cache_sb_v, ckv=cache_mla_ckv, kpe=cache_mla_kpe)
    y_sample, sn = _trunk(x_sample, c_sample, past_len + jnp.arange(x_sample.shape[1], dtype=jnp.int32),
                          state_rwkv_shift, state_rwkv_wkv, state_s5_re, state_s5_im, paged, P, prepared)
    return (y_prompt, y_sample,
            pn["sb_k"], pn["sb_v"], pn["ckv"], pn["kpe"], pn["wkv"], pn["shift"], pn["s5re"], pn["s5im"],
            sn["sb_k"], sn["sb_v"], sn["ckv"], sn["kpe"], sn["wkv"], sn["shift"], sn["s5re"], sn["s5im"])
```
